```python
import math
import jax, jax.numpy as jnp
from jax import lax
import numpy as np

D_MODEL = 1024
BATCH = 8
SEQ = 4096
DEPTH = 4

D_MIX = D_MODEL
N_HEADS = 8
N_KV_HEADS = 2
HEAD_DIM = 64
D_ATTN = N_HEADS * HEAD_DIM
D_KV = N_KV_HEADS * HEAD_DIM
D_SSM = D_MIX - D_ATTN
SSM_GROUP = 16
N_SSM_GROUPS = D_SSM // SSM_GROUP
SSM_STATE = 64
N_DIR = 2
WINDOW = 128
BLOCK = 128
N_BUCKETS = 32
MAX_DISTANCE = 128
D_FF = 2816
D_IN = D_ATTN + 2 * D_KV + D_SSM
EPS = 1e-6
NEG_INF = -1e30

kernel_name = "hymba_swa_s5_macaron_encoder"


def rms_norm(x, g):
    xf = x.astype(jnp.float32)
    y = xf * lax.rsqrt(jnp.mean(xf * xf, axis=-1, keepdims=True) + EPS)
    return (y * g.astype(jnp.float32)).astype(x.dtype)


def swiglu(x, w_gate, w_up, w_down):
    return (jax.nn.silu(x @ w_gate) * (x @ w_up)) @ w_down


def t5_bucket(rel):
    half = N_BUCKETS // 2
    max_exact = half // 2
    ret = jnp.where(rel > 0, half, 0)
    n = jnp.abs(rel)
    nf = jnp.maximum(n, 1).astype(jnp.float32)
    large = max_exact + (jnp.log(nf / max_exact) / math.log(MAX_DISTANCE / max_exact)
                         * (half - max_exact)).astype(jnp.int32)
    large = jnp.minimum(large, half - 1)
    return ret + jnp.where(n < max_exact, n, large)


def banded_attention(q, k, v, sink, bias):
    b, l = q.shape[0], q.shape[1]
    nb = l // BLOCK
    grp = N_HEADS // N_KV_HEADS
    qb = q.reshape(b, nb, BLOCK, N_KV_HEADS, grp, HEAD_DIM)

    def windows(t):
        tb = t.reshape(b, nb, BLOCK, N_KV_HEADS, HEAD_DIM)
        pad = jnp.pad(tb, ((0, 0), (1, 1), (0, 0), (0, 0), (0, 0)))
        return jnp.concatenate([pad[:, :-2], pad[:, 1:-1], pad[:, 2:]], axis=2)

    kw, vw = windows(k), windows(v)
    s = jnp.einsum('bnqkgd,bnskd->bnkgqs', qb, kw).astype(jnp.float32) * (HEAD_DIM ** -0.5)
    s = s + bias.astype(jnp.float32).reshape(N_KV_HEADS, grp, BLOCK, 3 * BLOCK)
    qpos = jnp.arange(BLOCK)[:, None]
    kpos = jnp.arange(3 * BLOCK)[None, :] - BLOCK
    in_band = jnp.abs(kpos - qpos) <= WINDOW
    kabs = jnp.arange(nb)[:, None] * BLOCK + kpos
    k_valid = (kabs >= 0) & (kabs < l)
    mask = in_band[None] & k_valid[:, None, :]
    s = jnp.where(mask[None, :, None, None], s, NEG_INF)
    sink_b = sink.astype(jnp.float32).reshape(N_KV_HEADS, grp, 1, 1)
    m = jnp.maximum(jnp.max(s, axis=-1, keepdims=True), sink_b)
    p = jnp.exp(s - m)
    denom = jnp.sum(p, axis=-1, keepdims=True) + jnp.exp(sink_b - m)
    p = (p / denom).astype(v.dtype)
    o = jnp.einsum('bnkgqs,bnskd->bnqkgd', p, vw)
    return o.reshape(b, l, D_ATTN)


def _scan_combine(e1, e2):
    a1r, a1i, b1r, b1i = e1
    a2r, a2i, b2r, b2i = e2
    return (a2r * a1r - a2i * a1i,
            a2r * a1i + a2i * a1r,
            a2r * b1r - a2i * b1i + b2r,
            a2r * b1i + a2i * b1r + b2i)


def s5_direction(u, a_re, a_im, log_dt, b_re, b_im, c_re, c_im, reverse):
    dt = jnp.exp(log_dt)[:, None]
    mag = jnp.exp(dt * a_re)
    ab_re = mag * jnp.cos(dt * a_im)
    ab_im = mag * jnp.sin(dt * a_im)
    den = a_re * a_re + a_im * a_im
    nr, ni = ab_re - 1.0, ab_im
    k_re = (nr * a_re + ni * a_im) / den
    k_im = (ni * a_re - nr * a_im) / den
    bb_re = k_re[..., None] * b_re - k_im[..., None] * b_im
    bb_im = k_re[..., None] * b_im + k_im[..., None] * b_re
    x_re = jnp.einsum('blgh,gph->blgp', u, bb_re)
    x_im = jnp.einsum('blgh,gph->blgp', u, bb_im)
    shape = x_re.shape
    elems = (jnp.broadcast_to(ab_re, shape), jnp.broadcast_to(ab_im, shape), x_re, x_im)
    _, _, s_re, s_im = lax.associative_scan(_scan_combine, elems, reverse=reverse, axis=1)
    return jnp.einsum('blgp,ghp->blgh', s_re, c_re) - jnp.einsum('blgp,ghp->blgh', s_im, c_im)


def s5_mixer(u, a_re, a_im, log_dt, b_re, b_im, c_re, c_im, d, w_glu, b_glu):
    bsz, l = u.shape[0], u.shape[1]
    f32 = jnp.float32
    uf = u.astype(f32)
    ug = uf.reshape(bsz, l, N_SSM_GROUPS, SSM_GROUP)
    y = jnp.zeros_like(ug)
    for di, rev in ((0, False), (1, True)):
        y = y + s5_direction(ug, a_re[di].astype(f32), a_im[di].astype(f32), log_dt[di].astype(f32),
                             b_re[di].astype(f32), b_im[di].astype(f32),
                             c_re[di].astype(f32), c_im[di].astype(f32), rev)
    y = y.reshape(bsz, l, D_SSM) + d.astype(f32) * uf
    z = jax.nn.gelu(y)
    out = z * jax.nn.sigmoid(z @ w_glu.astype(f32) + b_glu.astype(f32))
    return out.astype(u.dtype)


def setup_inputs(seed: int = 0) -> dict:
    key = jax.random.key(seed)
    ks = iter(jax.random.split(key, 40))
    f32 = jnp.float32

    def nrm(shape, scale):
        return jax.random.normal(next(ks), shape, f32) * scale

    def gain(shape):
        return 1.0 + 0.02 * jax.random.normal(next(ks), shape, f32)

    L, D, G, P, H = DEPTH, D_MODEL, N_SSM_GROUPS, SSM_STATE, SSM_GROUP
    a_re_init = -0.5 * jnp.ones((L, N_DIR, G, P), f32)
    a_im_init = jnp.broadcast_to(jnp.pi * jnp.arange(P, dtype=f32), (L, N_DIR, G, P))
    return {
        "x": nrm((BATCH, SEQ, D), 1.0),
        "rel_bias_table": nrm((N_BUCKETS, N_HEADS), 0.5),
        "ffn1_norm": gain((L, D)),
        "ffn1_w_gate": nrm((L, D, D_FF), D ** -0.5),
        "ffn1_w_up": nrm((L, D, D_FF), D ** -0.5),
        "ffn1_w_down": nrm((L, D_FF, D), D_FF ** -0.5),
        "mix_norm": gain((L, D)),
        "w_in": nrm((L, D, D_IN), D ** -0.5),
        "attn_sink": nrm((L, N_HEADS), 0.5),
        "ssm_a_re": a_re_init - 0.02 * jnp.abs(jax.random.normal(next(ks), (L, N_DIR, G, P), f32)),
        "ssm_a_im": a_im_init + 0.01 * jax.random.normal(next(ks), (L, N_DIR, G, P), f32),
        "ssm_log_dt": jax.random.uniform(next(ks), (L, N_DIR, G), f32, math.log(1e-3), math.log(1e-1)),
        "ssm_b_re": nrm((L, N_DIR, G, P, H), (2.0 * H) ** -0.5),
        "ssm_b_im": nrm((L, N_DIR, G, P, H), (2.0 * H) ** -0.5),
        "ssm_c_re": nrm((L, N_DIR, G, H, P), (2.0 * P) ** -0.5),
        "ssm_c_im": nrm((L, N_DIR, G, H, P), (2.0 * P) ** -0.5),
        "ssm_d": nrm((L, D_SSM), 1.0),
        "ssm_w_glu": nrm((L, D_SSM, D_SSM), D_SSM ** -0.5),
        "ssm_b_glu": nrm((L, D_SSM), 0.01),
        "attn_out_norm": gain((L, D_ATTN)),
        "ssm_out_norm": gain((L, D_SSM)),
        "w_out": nrm((L, D_MIX, D), D_MIX ** -0.5),
        "ffn2_norm": gain((L, D)),
        "ffn2_w_gate": nrm((L, D, D_FF), D ** -0.5),
        "ffn2_w_up": nrm((L, D, D_FF), D ** -0.5),
        "ffn2_w_down": nrm((L, D_FF, D), D_FF ** -0.5),
        "final_norm": gain((D,)),
    }


def reference(x, rel_bias_table, ffn1_norm, ffn1_w_gate, ffn1_w_up, ffn1_w_down, mix_norm, w_in,
              attn_sink, ssm_a_re, ssm_a_im, ssm_log_dt, ssm_b_re, ssm_b_im, ssm_c_re, ssm_c_im,
              ssm_d, ssm_w_glu, ssm_b_glu, attn_out_norm, ssm_out_norm, w_out, ffn2_norm,
              ffn2_w_gate, ffn2_w_up, ffn2_w_down, final_norm):
    bsz, l = x.shape[0], x.shape[1]
    rel = (jnp.arange(3 * BLOCK)[None, :] - BLOCK) - jnp.arange(BLOCK)[:, None]
    bias = jnp.transpose(rel_bias_table[t5_bucket(rel)], (2, 0, 1))
    split_pts = [D_ATTN, D_ATTN + D_KV, D_ATTN + 2 * D_KV]
    for i in range(DEPTH):
        x = x + 0.5 * swiglu(rms_norm(x, ffn1_norm[i]), ffn1_w_gate[i], ffn1_w_up[i], ffn1_w_down[i])
        h = rms_norm(x, mix_norm[i])
        proj = h @ w_in[i]
        q, k, v, u = jnp.split(proj, split_pts, axis=-1)
        attn = banded_attention(q.reshape(bsz, l, N_HEADS, HEAD_DIM),
                                k.reshape(bsz, l, N_KV_HEADS, HEAD_DIM),
                                v.reshape(bsz, l, N_KV_HEADS, HEAD_DIM),
                                attn_sink[i], bias)
        ssm = s5_mixer(u, ssm_a_re[i], ssm_a_im[i], ssm_log_dt[i], ssm_b_re[i], ssm_b_im[i],
                       ssm_c_re[i], ssm_c_im[i], ssm_d[i], ssm_w_glu[i], ssm_b_glu[i])
        mixed = jnp.concatenate([rms_norm(attn, attn_out_norm[i]), rms_norm(ssm, ssm_out_norm[i])], axis=-1)
        x = x + mixed @ w_out[i]
        x = x + 0.5 * swiglu(rms_norm(x, ffn2_norm[i]), ffn2_w_gate[i], ffn2_w_up[i], ffn2_w_down[i])
    return rms_norm(x, final_norm)
```

```python
import functools
import math

import jax
import jax.numpy as jnp
from jax import lax
from jax.experimental import pallas as pl
from jax.experimental.pallas import tpu as pltpu

EPS = 1e-6
NEG_INF = -1e30

N_HEADS = 8
N_KV_HEADS = 2
HEAD_DIM = 64
SSM_GROUP = 16
SSM_STATE = 64
ATT_BLOCK = 128
N_BUCKETS = 32
MAX_DISTANCE = 128

V7X_SUBLANES = 8
V7X_LANES = 128
V7X_MXU_DIM = 256
V7X_VMEM_BYTES = 64 * 1024 * 1024

F32 = jnp.float32
BF16 = jnp.bfloat16


def _vmem_limit(nbytes):
    return int(min(nbytes + (8 << 20), V7X_VMEM_BYTES - (6 << 20)))


def _rms(x, g):
    return x * lax.rsqrt(jnp.mean(x * x, axis=-1, keepdims=True) + EPS) * g


def _const_spec(shape):
    nd = len(shape)
    return pl.BlockSpec(shape, lambda *_: (0,) * nd, pipeline_mode=pl.Buffered(1))


def _ffn_kernel(x_ref, g_ref, wg_ref, wu_ref, wd_ref, *rest, n_chunks, final):
    if final:
        fg_ref, o_ref, acc_ref = rest
    else:
        o_ref, acc_ref = rest
    x = x_ref[...]
    h = _rms(x, g_ref[...]).astype(BF16)

    def chunk(c, carry):
        gate = jnp.dot(h, wg_ref[c], preferred_element_type=F32)
        up = jnp.dot(h, wu_ref[c], preferred_element_type=F32)
        act = (gate * jax.nn.sigmoid(gate) * up).astype(BF16)
        part = jnp.dot(act, wd_ref[c], preferred_element_type=F32)

        @pl.when(c == 0)
        def _():
            acc_ref[...] = part

        @pl.when(c > 0)
        def _():
            acc_ref[...] += part

        return carry

    lax.fori_loop(0, n_chunks, chunk, 0)
    y = x + 0.5 * acc_ref[...]
    if final:
        y = _rms(y, fg_ref[...])
    o_ref[...] = y


def _ffn(x2, g, wg, wu, wd, final_g=None, *, tm=512):
    t, d = x2.shape
    n_chunks, _, fc = wg.shape
    final = final_g is not None
    row = pl.BlockSpec((tm, d), lambda i: (i, 0))
    in_specs = [row, _const_spec((1, d)), _const_spec(wg.shape), _const_spec(wu.shape), _const_spec(wd.shape)]
    args = [x2, g, wg, wu, wd]
    if final:
        in_specs.append(_const_spec((1, d)))
        args.append(final_g)
    wbytes = 2 * (wg.size + wu.size + wd.size)
    est = wbytes + 5 * tm * d * 4 + 4 * tm * fc * 4
    return pl.pallas_call(
        functools.partial(_ffn_kernel, n_chunks=n_chunks, final=final),
        out_shape=jax.ShapeDtypeStruct((t, d), F32),
        grid=(t // tm,),
        in_specs=in_specs,
        out_specs=row,
        scratch_shapes=[pltpu.VMEM((tm, d), F32)],
        compiler_params=pltpu.CompilerParams(
            dimension_semantics=("arbitrary",), vmem_limit_bytes=_vmem_limit(est)),
        name="ffn",
    )(*args)


def _proj_kernel(x_ref, g_ref, w_ref, q_ref, kv_ref, u_ref, *, d_attn, d_kv2):
    h = _rms(x_ref[0], g_ref[...]).astype(BF16)
    p = jnp.dot(h, w_ref[...], preferred_element_type=F32)
    q_ref[0] = (p[:, :d_attn] * (HEAD_DIM ** -0.5)).astype(BF16)
    kv_ref[0] = p[:, d_attn:d_attn + d_kv2].astype(BF16)
    u_ref[...] = p[:, d_attn + d_kv2:]


def _proj(x3, g, w_in, *, d_attn, d_kv2, tm=512):
    b, l, d = x3.shape
    d_in = w_in.shape[1]
    d_ssm = d_in - d_attn - d_kv2
    est = 2 * w_in.size + 2 * tm * d * 4 + 3 * tm * d_in * 4
    return pl.pallas_call(
        functools.partial(_proj_kernel, d_attn=d_attn, d_kv2=d_kv2),
        out_shape=(
            jax.ShapeDtypeStruct((b, l, d_attn), BF16),
            jax.ShapeDtypeStruct((b, l, d_kv2), BF16),
            jax.ShapeDtypeStruct((l, b * d_ssm), F32),
        ),
        grid=(b, l // tm),
        in_specs=[pl.BlockSpec((1, tm, d), lambda bi, i: (bi, i, 0)),
                  _const_spec((1, d)), _const_spec(w_in.shape)],
        out_specs=(
            pl.BlockSpec((1, tm, d_attn), lambda bi, i: (bi, i, 0)),
            pl.BlockSpec((1, tm, d_kv2), lambda bi, i: (bi, i, 0)),
            pl.BlockSpec((tm, d_ssm), lambda bi, i: (i, bi)),
        ),
        compiler_params=pltpu.CompilerParams(
            dimension_semantics=("arbitrary", "arbitrary"), vmem_limit_bytes=_vmem_limit(est)),
        name="mix_proj",
    )(x3, g, w_in)


def _bias_kernel(table_ref, idx_ref, band_ref, o_ref):
    idx = idx_ref[...]
    band = band_ref[...] > 0
    for h in range(N_HEADS):
        acc = jnp.zeros(idx.shape, F32)
        for bkt in range(N_BUCKETS):
            acc = jnp.where(idx == bkt, table_ref[bkt, h], acc)
        o_ref[h] = jnp.where(band, acc, NEG_INF)


def _t5_bucket(rel):
    half = N_BUCKETS // 2
    max_exact = half // 2
    ret = jnp.where(rel > 0, half, 0)
    n = jnp.abs(rel)
    nf = jnp.maximum(n, 1).astype(F32)
    large = max_exact + (jnp.log(nf / max_exact) / math.log(MAX_DISTANCE / max_exact)
                         * (half - max_exact)).astype(jnp.int32)
    large = jnp.minimum(large, half - 1)
    return ret + jnp.where(n < max_exact, n, large)


def _rel_bias(table):
    rel = (jnp.arange(3 * ATT_BLOCK)[None, :] - ATT_BLOCK) - jnp.arange(ATT_BLOCK)[:, None]
    idx = _t5_bucket(rel).astype(jnp.int32)
    band = (jnp.abs(rel) <= ATT_BLOCK).astype(jnp.int32)
    return pl.pallas_call(
        _bias_kernel,
        out_shape=jax.ShapeDtypeStruct((N_HEADS, ATT_BLOCK, 3 * ATT_BLOCK), F32),
        in_specs=[pl.BlockSpec(memory_space=pltpu.SMEM),
                  pl.BlockSpec(memory_space=pltpu.VMEM),
                  pl.BlockSpec(memory_space=pltpu.VMEM)],
        out_specs=pl.BlockSpec(memory_space=pltpu.VMEM),
        name="rel_bias",
    )(table, idx, band)


def _attn_kernel(sink_ref, q_ref, kvp_ref, kvm_ref, kvn_ref, bias_ref, g_ref, o_ref, kv_scr, *, nq):
    i = pl.program_id(1)
    n_i = pl.num_programs(1)
    blk = ATT_BLOCK
    d_kv = N_KV_HEADS * HEAD_DIM
    grp = N_HEADS // N_KV_HEADS
    kv_scr[0:blk] = kvp_ref[0]
    kv_scr[blk:blk + nq * blk] = kvm_ref[0]
    kv_scr[blk + nq * blk:] = kvn_ref[0]
    col = lax.broadcasted_iota(jnp.int32, (1, 3 * blk), 1)
    edge_first = jnp.where(col < blk, jnp.where(i == 0, NEG_INF, 0.0), 0.0)
    edge_last = jnp.where(col >= 2 * blk, jnp.where(i == n_i - 1, NEG_INF, 0.0), 0.0)
    for jb in range(nq):
        q_blk = q_ref[0, jb * blk:(jb + 1) * blk, :]
        kv_win = kv_scr[jb * blk:(jb + 3) * blk, :]
        outs = []
        for h in range(N_HEADS):
            kh = h // grp
            q_h = q_blk[:, h * HEAD_DIM:(h + 1) * HEAD_DIM]
            k_h = kv_win[:, kh * HEAD_DIM:(kh + 1) * HEAD_DIM]
            v_h = kv_win[:, d_kv + kh * HEAD_DIM:d_kv + (kh + 1) * HEAD_DIM]
            s = lax.dot_general(q_h, k_h, (((1,), (1,)), ((), ())), preferred_element_type=F32)
            s = s + bias_ref[h]
            if jb == 0:
                s = s + edge_first
            if jb == nq - 1:
                s = s + edge_last
            sink = sink_ref[h]
            m = jnp.maximum(jnp.max(s, axis=-1, keepdims=True), sink)
            p = jnp.exp(s - m)
            denom = jnp.sum(p, axis=-1, keepdims=True) + jnp.exp(sink - m)
            o_h = jnp.dot(p.astype(BF16), v_h, preferred_element_type=F32)
            outs.append(o_h / denom)
        o = jnp.concatenate(outs, axis=-1)
        o_ref[0, jb * blk:(jb + 1) * blk, :] = _rms(o, g_ref[...]).astype(BF16)


def _attention(q, kv, bias, sink, g, *, nq=4):
    b, l, d_attn = q.shape
    d_kv2 = kv.shape[-1]
    blk = ATT_BLOCK
    tq = nq * blk
    nb = l // blk
    est = 2 * bias.size * 4 + 6 * tq * (d_attn + d_kv2) * 2 + 16 * blk * 3 * blk * 4
    return pl.pallas_call(
        functools.partial(_attn_kernel, nq=nq),
        out_shape=jax.ShapeDtypeStruct((b, l, d_attn), BF16),
        grid=(b, l // tq),
        in_specs=[
            pl.BlockSpec(memory_space=pltpu.SMEM),
            pl.BlockSpec((1, tq, d_attn), lambda bi, i: (bi, i, 0)),
            pl.BlockSpec((1, blk, d_kv2), lambda bi, i: (bi, jnp.maximum(i * nq - 1, 0), 0)),
            pl.BlockSpec((1, tq, d_kv2), lambda bi, i: (bi, i, 0)),
            pl.BlockSpec((1, blk, d_kv2), lambda bi, i: (bi, jnp.minimum((i + 1) * nq, nb - 1), 0)),
            _const_spec(bias.shape),
            _const_spec((1, d_attn)),
        ],
        out_specs=pl.BlockSpec((1, tq, d_attn), lambda bi, i: (bi, i, 0)),
        scratch_shapes=[pltpu.VMEM((tq + 2 * blk, d_kv2), BF16)],
        compiler_params=pltpu.CompilerParams(
            dimension_semantics=("arbitrary", "arbitrary"), vmem_limit_bytes=_vmem_limit(est)),
        name="band_attn",
    )(sink, q, kv, kv, kv, bias, g)


def _ssm_kernel(u_ref, a_ref, wb_ref, wc_ref, y_ref, x_scr, s_scr, st_scr, *, tc, nb, reverse, lane_tile):
    n_half, _, n2 = a_ref.shape
    nre = n2 // 2
    kh = wb_ref.shape[1]
    rows2 = 2 * nb
    n_pairs = tc // 2

    @pl.when(pl.program_id(0) == 0)
    def _():
        st_scr[...] = jnp.zeros_like(st_scr)

    for hf in range(n_half):
        u_h = u_ref[:, hf * kh:(hf + 1) * kh].astype(BF16)
        x_scr[...] = jnp.dot(u_h, wb_ref[hf], preferred_element_type=F32)
        for lt in range(nre // lane_tile):
            re_sl = slice(lt * lane_tile, (lt + 1) * lane_tile)
            im_sl = slice(nre + lt * lane_tile, nre + (lt + 1) * lane_tile)
            a_re = a_ref[hf, :, re_sl]
            a_im = a_ref[hf, :, im_sl]

            def step(s_re, s_im, x_re, x_im):
                n_re = a_re * s_re - a_im * s_im + x_re
                n_im = a_re * s_im + a_im * s_re + x_im
                return n_re, n_im

            def pair(j, carry):
                s_re, s_im = carry
                jj = (n_pairs - 1 - j) if reverse else j
                r0 = pl.multiple_of(jj * rows2, rows2)
                x_re = x_scr[pl.ds(r0, rows2), re_sl]
                x_im = x_scr[pl.ds(r0, rows2), im_sl]
                first, second = (slice(nb, rows2), slice(0, nb)) if reverse else (slice(0, nb), slice(nb, rows2))
                s1_re, s1_im = step(s_re, s_im, x_re[first], x_im[first])
                s2_re, s2_im = step(s1_re, s1_im, x_re[second], x_im[second])
                lo_re, hi_re = (s2_re, s1_re) if reverse else (s1_re, s2_re)
                lo_im, hi_im = (s2_im, s1_im) if reverse else (s1_im, s2_im)
                s_scr[pl.ds(r0, rows2), re_sl] = jnp.concatenate([lo_re, hi_re], axis=0).astype(BF16)
                s_scr[pl.ds(r0, rows2), im_sl] = jnp.concatenate([lo_im, hi_im], axis=0).astype(BF16)
                return s2_re, s2_im

            init = (st_scr[hf, :, re_sl], st_scr[hf, :, im_sl])
            f_re, f_im = lax.fori_loop(0, n_pairs, pair, init, unroll=2)
            st_scr[hf, :, re_sl] = f_re
            st_scr[hf, :, im_sl] = f_im
        y_ref[:, hf * kh:(hf + 1) * kh] = jnp.dot(s_scr[...], wc_ref[hf], preferred_element_type=F32)


def _ssm_scan(u_tm, a, wb, wc, *, nb, reverse, tc=64, lane_tile=512):
    rows, d_ssm = u_tm.shape
    n_half, kh, n2 = wb.shape
    m = tc * nb
    n_t = rows // m
    idx = (lambda i: (n_t - 1 - i, 0)) if reverse else (lambda i: (i, 0))
    est = 4 * m * d_ssm * 4 + m * n2 * 6 + 4 * (wb.size + wc.size) + 3 * a.size * 4
    return pl.pallas_call(
        functools.partial(_ssm_kernel, tc=tc, nb=nb, reverse=reverse, lane_tile=lane_tile),
        out_shape=jax.ShapeDtypeStruct((rows, d_ssm), F32),
        grid=(n_t,),
        in_specs=[pl.BlockSpec((m, d_ssm), idx),
                  _const_spec(a.shape), _const_spec(wb.shape), _const_spec(wc.shape)],
        out_specs=pl.BlockSpec((m, d_ssm), idx),
        scratch_shapes=[pltpu.VMEM((m, n2), F32),
                        pltpu.VMEM((m, n2), BF16),
                        pltpu.VMEM((n_half, nb, n2), F32)],
        compiler_params=pltpu.CompilerParams(
            dimension_semantics=("arbitrary",), vmem_limit_bytes=_vmem_limit(est)),
        name="s5_scan_bwd" if reverse else "s5_scan_fwd",
    )(u_tm, a, wb, wc)


def _ssm_params(a_re, a_im, log_dt, b_re, b_im, c_re, c_im, nb):
    g, p = a_re.shape
    h = b_re.shape[-1]
    n_half = (g * h) // V7X_MXU_DIM
    gh = g // n_half
    dt = jnp.exp(log_dt)[:, None]
    mag = jnp.exp(dt * a_re)
    ab_re = mag * jnp.cos(dt * a_im)
    ab_im = mag * jnp.sin(dt * a_im)
    den = a_re * a_re + a_im * a_im
    nr, ni = ab_re - 1.0, ab_im
    k_re = (nr * a_re + ni * a_im) / den
    k_im = (ni * a_re - nr * a_im) / den
    bb_re = k_re[..., None] * b_re - k_im[..., None] * b_im
    bb_im = k_re[..., None] * b_im + k_im[..., None] * b_re
    eye = jnp.eye(gh, dtype=F32)

    def in_proj(bb):
        bb = bb.reshape(n_half, gh, p, h)
        w = jnp.einsum('ngph,gk->nghkp', bb, eye)
        return w.reshape(n_half, gh * h, gh * p)

    def out_proj(c):
        c = c.reshape(n_half, gh, h, p)
        w = jnp.einsum('nghp,gk->ngpkh', c, eye)
        return w.reshape(n_half, gh * p, gh * h)

    wb = jnp.concatenate([in_proj(bb_re), in_proj(bb_im)], axis=-1).astype(BF16)
    wc = jnp.concatenate([out_proj(c_re), out_proj(-c_im)], axis=1).astype(BF16)
    a = jnp.concatenate([ab_re.reshape(n_half, gh * p), ab_im.reshape(n_half, gh * p)], axis=-1)
    a = jnp.broadcast_to(a[:, None, :], (n_half, nb, 2 * gh * p))
    return a, wb, wc


def _mix_out_kernel(x_ref, at_ref, yf_ref, yb_ref, u_ref, d_ref, wglu_ref, bglu_ref, gs_ref,
                    woa_ref, wos_ref, o_ref):
    y = yf_ref[...] + yb_ref[...] + d_ref[...] * u_ref[...]
    z = jax.nn.gelu(y)
    gate = jnp.dot(z.astype(BF16), wglu_ref[...], preferred_element_type=F32) + bglu_ref[...]
    s = z * jax.nn.sigmoid(gate)
    s_n = _rms(s, gs_ref[...]).astype(BF16)
    mixed = (jnp.dot(at_ref[0], woa_ref[...], preferred_element_type=F32)
             + jnp.dot(s_n, wos_ref[...], preferred_element_type=F32))
    o_ref[0] = x_ref[0] + mixed


def _mix_out(x3, attn_n, y_f, y_b, u_tm, d, w_glu, b_glu, g_s, wo_a, wo_s, *, tm=512):
    b, l, dm = x3.shape
    d_attn = attn_n.shape[-1]
    d_ssm = w_glu.shape[0]
    xrow = pl.BlockSpec((1, tm, dm), lambda bi, i: (bi, i, 0))
    tmaj = pl.BlockSpec((tm, d_ssm), lambda bi, i: (i, bi))
    est = 2 * (w_glu.size + wo_a.size + wo_s.size) + 4 * tm * dm * 4 + 12 * tm * d_ssm * 4
    return pl.pallas_call(
        _mix_out_kernel,
        out_shape=jax.ShapeDtypeStruct((b, l, dm), F32),
        grid=(b, l // tm),
        in_specs=[xrow,
                  pl.BlockSpec((1, tm, d_attn), lambda bi, i: (bi, i, 0)),
                  tmaj, tmaj, tmaj,
                  _const_spec((1, d_ssm)), _const_spec(w_glu.shape), _const_spec((1, d_ssm)),
                  _const_spec((1, d_ssm)), _const_spec(wo_a.shape), _const_spec(wo_s.shape)],
        out_specs=xrow,
        compiler_params=pltpu.CompilerParams(
            dimension_semantics=("arbitrary", "arbitrary"), vmem_limit_bytes=_vmem_limit(est)),
        name="mix_out",
    )(x3, attn_n, y_f, y_b, u_tm, d, w_glu, b_glu, g_s, wo_a, wo_s)


def _ffn_weights(w_gate, w_up, w_down, fc=V7X_MXU_DIM):
    d, dff = w_gate.shape
    n = dff // fc
    wg = w_gate.astype(BF16).reshape(d, n, fc).transpose(1, 0, 2)
    wu = w_up.astype(BF16).reshape(d, n, fc).transpose(1, 0, 2)
    wd = w_down.astype(BF16).reshape(n, fc, d)
    return wg, wu, wd


def kernel(x, rel_bias_table, ffn1_norm, ffn1_w_gate, ffn1_w_up, ffn1_w_down, mix_norm, w_in, attn_sink, ssm_a_re, ssm_a_im, ssm_log_dt, ssm_b_re, ssm_b_im, ssm_c_re, ssm_c_im, ssm_d, ssm_w_glu, ssm_b_glu, attn_out_norm, ssm_out_norm, w_out, ffn2_norm, ffn2_w_gate, ffn2_w_up, ffn2_w_down, final_norm):
    b, l, dm = x.shape
    depth = w_in.shape[0]
    d_attn = N_HEADS * HEAD_DIM
    d_kv2 = 2 * N_KV_HEADS * HEAD_DIM
    d_ssm = ssm_d.shape[-1]
    assert b == V7X_SUBLANES, "the scan keeps one batch sequence per sublane"

    bias = _rel_bias(rel_bias_table.astype(F32))
    row = lambda v: v.astype(F32).reshape(1, -1)

    for i in range(depth):
        x2 = _ffn(x.reshape(b * l, dm), row(ffn1_norm[i]),
                  *_ffn_weights(ffn1_w_gate[i], ffn1_w_up[i], ffn1_w_down[i]))
        x = x2.reshape(b, l, dm)

        q, kv, u_t = _proj(x, row(mix_norm[i]), w_in[i].astype(BF16), d_attn=d_attn, d_kv2=d_kv2)
        attn_n = _attention(q, kv, bias, attn_sink[i].astype(F32), row(attn_out_norm[i]))

        u_tm = u_t.reshape(l * b, d_ssm)
        ys = []
        for di in range(2):
            a, wb, wc = _ssm_params(ssm_a_re[i, di], ssm_a_im[i, di], ssm_log_dt[i, di],
                                    ssm_b_re[i, di], ssm_b_im[i, di], ssm_c_re[i, di], ssm_c_im[i, di], b)
            y = _ssm_scan(u_tm, a, wb, wc, nb=b, reverse=(di == 1))
            ys.append(y.reshape(l, b * d_ssm))

        wo = w_out[i].astype(BF16)
        x = _mix_out(x, attn_n, ys[0], ys[1], u_t, row(ssm_d[i]), ssm_w_glu[i].astype(BF16),
                     row(ssm_b_glu[i]), row(ssm_out_norm[i]), wo[:d_attn], wo[d_attn:])

        fin = row(final_norm) if i == depth - 1 else None
        x2 = _ffn(x.reshape(b * l, dm), row(ffn2_norm[i]),
                  *_ffn_weights(ffn2_w_gate[i], ffn2_w_up[i], ffn2_w_down[i]), fin)
        x = x2.reshape(b, l, dm)
    return x
```

```python
import functools
import math

import jax
import jax.numpy as jnp
from jax import lax
from jax.experimental import pallas as pl
from jax.experimental.pallas import tpu as pltpu

EPS = 1e-6
NEG_INF = -1e30

N_HEADS = 8
N_KV_HEADS = 2
HEAD_DIM = 64
SSM_GROUP = 16
SSM_STATE = 64
ATT_BLOCK = 128
N_BUCKETS = 32
MAX_DISTANCE = 128

V7X_SUBLANES = 8
V7X_LANES = 128
V7X_MXU_DIM = 256
V7X_VMEM_BYTES = 64 * 1024 * 1024

F32 = jnp.float32
BF16 = jnp.bfloat16


def _vmem_limit(nbytes):
    return int(min(nbytes + (8 << 20), V7X_VMEM_BYTES - (6 << 20)))


def _rms(x, g):
    return x * lax.rsqrt(jnp.mean(x * x, axis=-1, keepdims=True) + EPS) * g


def _const_spec(shape):
    nd = len(shape)
    return pl.BlockSpec(shape, lambda *_: (0,) * nd, pipeline_mode=pl.Buffered(1))


def _ffn_kernel(x_ref, g_ref, wgu_ref, wd_ref, *rest, n_chunks, final):
    if final:
        fg_ref, o_ref, h_scr, gu_scr = rest
    else:
        o_ref, h_scr, gu_scr = rest
    fc = wd_ref.shape[1]
    x = x_ref[...]
    h_scr[...] = _rms(x, g_ref[...]).astype(BF16)
    o_ref[...] = x

    def gate_up(c, slot):
        gu_scr[slot] = jnp.dot(h_scr[...], wgu_ref[c], preferred_element_type=F32)

    def down(c, slot):
        gu = gu_scr[slot]
        gate, up = gu[:, :fc], gu[:, fc:]
        act = (gate * jax.nn.sigmoid(gate) * up).astype(BF16)
        o_ref[...] += jnp.dot(act, wd_ref[c], preferred_element_type=F32)

    gate_up(0, 0)

    def two_chunks(k, carry):
        c = 2 * k
        gate_up(c + 1, 1)
        down(c, 0)
        gate_up(c + 2, 0)
        down(c + 1, 1)
        return carry

    assert n_chunks % 2 == 1
    lax.fori_loop(0, (n_chunks - 1) // 2, two_chunks, 0)
    down(n_chunks - 1, 0)
    if final:
        o_ref[...] = _rms(o_ref[...], fg_ref[...])


def _ffn(x2, g, wgu, wd, final_g=None, *, tm=512):
    t, d = x2.shape
    n_chunks, _, fc2 = wgu.shape
    final = final_g is not None
    row = pl.BlockSpec((tm, d), lambda i: (i, 0))
    in_specs = [row, _const_spec((1, d)), _const_spec(wgu.shape), _const_spec(wd.shape)]
    args = [x2, g, wgu, wd]
    if final:
        in_specs.append(_const_spec((1, d)))
        args.append(final_g)
    est = 2 * (wgu.size + wd.size) + 4 * tm * d * 4 + tm * d * 2 + 2 * tm * fc2 * 4
    return pl.pallas_call(
        functools.partial(_ffn_kernel, n_chunks=n_chunks, final=final),
        out_shape=jax.ShapeDtypeStruct((t, d), F32),
        grid=(t // tm,),
        in_specs=in_specs,
        out_specs=row,
        scratch_shapes=[pltpu.VMEM((tm, d), BF16), pltpu.VMEM((2, tm, fc2), F32)],
        compiler_params=pltpu.CompilerParams(
            dimension_semantics=("arbitrary",), vmem_limit_bytes=_vmem_limit(est)),
        name="ffn",
    )(*args)


def _proj_kernel(x_ref, g_ref, w_ref, q_ref, kv_ref, u_ref, *, d_attn, d_kv2):
    h = _rms(x_ref[0], g_ref[...]).astype(BF16)
    p = jnp.dot(h, w_ref[...], preferred_element_type=F32)
    q_ref[0] = (p[:, :d_attn] * (HEAD_DIM ** -0.5)).astype(BF16)
    kv_ref[0] = p[:, d_attn:d_attn + d_kv2].astype(BF16)
    u_ref[0] = p[:, d_attn + d_kv2:]


def _proj(x3, g, w_in, *, d_attn, d_kv2, tm=512):
    b, l, d = x3.shape
    d_in = w_in.shape[1]
    d_ssm = d_in - d_attn - d_kv2
    est = 2 * w_in.size + 2 * tm * d * 4 + 3 * tm * d_in * 4
    return pl.pallas_call(
        functools.partial(_proj_kernel, d_attn=d_attn, d_kv2=d_kv2),
        out_shape=(
            jax.ShapeDtypeStruct((b, l, d_attn), BF16),
            jax.ShapeDtypeStruct((b, l, d_kv2), BF16),
            jax.ShapeDtypeStruct((b, l, d_ssm), F32),
        ),
        grid=(b, l // tm),
        in_specs=[pl.BlockSpec((1, tm, d), lambda bi, i: (bi, i, 0)),
                  _const_spec((1, d)), _const_spec(w_in.shape)],
        out_specs=(
            pl.BlockSpec((1, tm, d_attn), lambda bi, i: (bi, i, 0)),
            pl.BlockSpec((1, tm, d_kv2), lambda bi, i: (bi, i, 0)),
            pl.BlockSpec((1, tm, d_ssm), lambda bi, i: (bi, i, 0)),
        ),
        compiler_params=pltpu.CompilerParams(
            dimension_semantics=("arbitrary", "arbitrary"), vmem_limit_bytes=_vmem_limit(est)),
        name="mix_proj",
    )(x3, g, w_in)


def _bias_kernel(table_ref, idx_ref, band_ref, o_ref):
    idx = idx_ref[...]
    band = band_ref[...] > 0
    for h in range(N_HEADS):
        acc = jnp.zeros(idx.shape, F32)
        for bkt in range(N_BUCKETS):
            acc = jnp.where(idx == bkt, table_ref[bkt, h], acc)
        o_ref[h] = jnp.where(band, acc, NEG_INF)


def _t5_bucket(rel):
    half = N_BUCKETS // 2
    max_exact = half // 2
    ret = jnp.where(rel > 0, half, 0)
    n = jnp.abs(rel)
    nf = jnp.maximum(n, 1).astype(F32)
    large = max_exact + (jnp.log(nf / max_exact) / math.log(MAX_DISTANCE / max_exact)
                         * (half - max_exact)).astype(jnp.int32)
    large = jnp.minimum(large, half - 1)
    return ret + jnp.where(n < max_exact, n, large)


def _rel_bias(table):
    rel = (jnp.arange(3 * ATT_BLOCK)[None, :] - ATT_BLOCK) - jnp.arange(ATT_BLOCK)[:, None]
    idx = _t5_bucket(rel).astype(jnp.int32)
    band = (jnp.abs(rel) <= ATT_BLOCK).astype(jnp.int32)
    return pl.pallas_call(
        _bias_kernel,
        out_shape=jax.ShapeDtypeStruct((N_HEADS, ATT_BLOCK, 3 * ATT_BLOCK), F32),
        in_specs=[pl.BlockSpec(memory_space=pltpu.SMEM),
                  pl.BlockSpec(memory_space=pltpu.VMEM),
                  pl.BlockSpec(memory_space=pltpu.VMEM)],
        out_specs=pl.BlockSpec(memory_space=pltpu.VMEM),
        name="rel_bias",
    )(table, idx, band)


def _attn_kernel(sink_ref, q_ref, kvp_ref, kvm_ref, kvn_ref, bias_ref, g_ref, o_ref, kv_scr, *, nq):
    i = pl.program_id(1)
    n_i = pl.num_programs(1)
    blk = ATT_BLOCK
    d_kv = N_KV_HEADS * HEAD_DIM
    grp = N_HEADS // N_KV_HEADS
    kv_scr[0:blk] = kvp_ref[0]
    kv_scr[blk:blk + nq * blk] = kvm_ref[0]
    kv_scr[blk + nq * blk:] = kvn_ref[0]
    col = lax.broadcasted_iota(jnp.int32, (1, 3 * blk), 1)
    edge_first = jnp.where(col < blk, jnp.where(i == 0, NEG_INF, 0.0), 0.0)
    edge_last = jnp.where(col >= 2 * blk, jnp.where(i == n_i - 1, NEG_INF, 0.0), 0.0)
    for jb in range(nq):
        q_blk = q_ref[0, jb * blk:(jb + 1) * blk, :]
        kv_win = kv_scr[jb * blk:(jb + 3) * blk, :]
        outs = []
        for h in range(N_HEADS):
            kh = h // grp
            q_h = q_blk[:, h * HEAD_DIM:(h + 1) * HEAD_DIM]
            k_h = kv_win[:, kh * HEAD_DIM:(kh + 1) * HEAD_DIM]
            v_h = kv_win[:, d_kv + kh * HEAD_DIM:d_kv + (kh + 1) * HEAD_DIM]
            s = lax.dot_general(q_h, k_h, (((1,), (1,)), ((), ())), preferred_element_type=F32)
            s = s + bias_ref[h]
            if jb == 0:
                s = s + edge_first
            if jb == nq - 1:
                s = s + edge_last
            sink = sink_ref[h]
            m = jnp.maximum(jnp.max(s, axis=-1, keepdims=True), sink)
            p = jnp.exp(s - m)
            denom = jnp.sum(p, axis=-1, keepdims=True) + jnp.exp(sink - m)
            o_h = jnp.dot(p.astype(BF16), v_h, preferred_element_type=F32)
            outs.append(o_h / denom)
        o = jnp.concatenate(outs, axis=-1)
        o_ref[0, jb * blk:(jb + 1) * blk, :] = _rms(o, g_ref[...]).astype(BF16)


def _attention(q, kv, bias, sink, g, *, nq=4):
    b, l, d_attn = q.shape
    d_kv2 = kv.shape[-1]
    blk = ATT_BLOCK
    tq = nq * blk
    nb = l // blk
    est = 2 * bias.size * 4 + 6 * tq * (d_attn + d_kv2) * 2 + 16 * blk * 3 * blk * 4
    return pl.pallas_call(
        functools.partial(_attn_kernel, nq=nq),
        out_shape=jax.ShapeDtypeStruct((b, l, d_attn), BF16),
        grid=(b, l // tq),
        in_specs=[
            pl.BlockSpec(memory_space=pltpu.SMEM),
            pl.BlockSpec((1, tq, d_attn), lambda bi, i: (bi, i, 0)),
            pl.BlockSpec((1, blk, d_kv2), lambda bi, i: (bi, jnp.maximum(i * nq - 1, 0), 0)),
            pl.BlockSpec((1, tq, d_kv2), lambda bi, i: (bi, i, 0)),
            pl.BlockSpec((1, blk, d_kv2), lambda bi, i: (bi, jnp.minimum((i + 1) * nq, nb - 1), 0)),
            _const_spec(bias.shape),
            _const_spec((1, d_attn)),
        ],
        out_specs=pl.BlockSpec((1, tq, d_attn), lambda bi, i: (bi, i, 0)),
        scratch_shapes=[pltpu.VMEM((tq + 2 * blk, d_kv2), BF16)],
        compiler_params=pltpu.CompilerParams(
            dimension_semantics=("arbitrary", "arbitrary"), vmem_limit_bytes=_vmem_limit(est)),
        name="band_attn",
    )(sink, q, kv, kv, kv, bias, g)


def _ssm_kernel(u_ref, a_ref, wb_ref, wc_ref, y_ref, ut_scr, yt_scr, x_scr, s_scr, st_scr,
                *, tc, nb, reverse, lane_tile):
    n_half, _, n2 = a_ref.shape
    nre = n2 // 2
    kh = wb_ref.shape[1]
    rows2 = 2 * nb
    n_pairs = tc // 2
    n_slab = ut_scr.shape[0]
    slab_per_half = kh // V7X_LANES

    @pl.when(pl.program_id(0) == 0)
    def _():
        st_scr[...] = jnp.zeros_like(st_scr)

    for bi in range(nb):
        for sl in range(n_slab):
            ut_scr.at[sl][pl.ds(bi, tc, stride=nb), :] = u_ref[bi, :, sl * V7X_LANES:(sl + 1) * V7X_LANES]

    for hf in range(n_half):
        u_h = jnp.concatenate([ut_scr[hf * slab_per_half + k] for k in range(slab_per_half)], axis=-1)
        x_scr[...] = jnp.dot(u_h.astype(BF16), wb_ref[hf], preferred_element_type=F32)
        for lt in range(nre // lane_tile):
            re_sl = slice(lt * lane_tile, (lt + 1) * lane_tile)
            im_sl = slice(nre + lt * lane_tile, nre + (lt + 1) * lane_tile)
            a_re = a_ref[hf, :, re_sl]
            a_im = a_ref[hf, :, im_sl]

            def step(s_re, s_im, x_re, x_im):
                n_re = a_re * s_re - a_im * s_im + x_re
                n_im = a_re * s_im + a_im * s_re + x_im
                return n_re, n_im

            def pair(j, carry):
                s_re, s_im = carry
                jj = (n_pairs - 1 - j) if reverse else j
                r0 = pl.multiple_of(jj * rows2, rows2)
                x_re = x_scr[pl.ds(r0, rows2), re_sl]
                x_im = x_scr[pl.ds(r0, rows2), im_sl]
                first, second = (slice(nb, rows2), slice(0, nb)) if reverse else (slice(0, nb), slice(nb, rows2))
                s1_re, s1_im = step(s_re, s_im, x_re[first], x_im[first])
                s2_re, s2_im = step(s1_re, s1_im, x_re[second], x_im[second])
                lo_re, hi_re = (s2_re, s1_re) if reverse else (s1_re, s2_re)
                lo_im, hi_im = (s2_im, s1_im) if reverse else (s1_im, s2_im)
                s_scr[pl.ds(r0, rows2), re_sl] = jnp.concatenate([lo_re, hi_re], axis=0).astype(BF16)
                s_scr[pl.ds(r0, rows2), im_sl] = jnp.concatenate([lo_im, hi_im], axis=0).astype(BF16)
                return s2_re, s2_im

            init = (st_scr[hf, :, re_sl], st_scr[hf, :, im_sl])
            f_re, f_im = lax.fori_loop(0, n_pairs, pair, init, unroll=2)
            st_scr[hf, :, re_sl] = f_re
            st_scr[hf, :, im_sl] = f_im
        y_h = jnp.dot(s_scr[...], wc_ref[hf], preferred_element_type=F32)
        for k in range(slab_per_half):
            yt_scr[hf * slab_per_half + k] = y_h[:, k * V7X_LANES:(k + 1) * V7X_LANES]

    for bi in range(nb):
        for sl in range(n_slab):
            y_ref[bi, :, sl * V7X_LANES:(sl + 1) * V7X_LANES] = yt_scr.at[sl][pl.ds(bi, tc, stride=nb), :]


def _ssm_scan(u, a, wb, wc, *, reverse, tc=64, lane_tile=512):
    nb, l, d_ssm = u.shape
    n_half, kh, n2 = wb.shape
    m = tc * nb
    n_t = l // tc
    idx = (lambda i: (0, n_t - 1 - i, 0)) if reverse else (lambda i: (0, i, 0))
    est = 6 * m * d_ssm * 4 + m * n2 * 6 + 2 * (wb.size + wc.size) + 3 * a.size * 4
    return pl.pallas_call(
        functools.partial(_ssm_kernel, tc=tc, nb=nb, reverse=reverse, lane_tile=lane_tile),
        out_shape=jax.ShapeDtypeStruct((nb, l, d_ssm), F32),
        grid=(n_t,),
        in_specs=[pl.BlockSpec((nb, tc, d_ssm), idx),
                  _const_spec(a.shape), _const_spec(wb.shape), _const_spec(wc.shape)],
        out_specs=pl.BlockSpec((nb, tc, d_ssm), idx),
        scratch_shapes=[pltpu.VMEM((d_ssm // V7X_LANES, m, V7X_LANES), F32),
                        pltpu.VMEM((d_ssm // V7X_LANES, m, V7X_LANES), F32),
                        pltpu.VMEM((m, n2), F32),
                        pltpu.VMEM((m, n2), BF16),
                        pltpu.VMEM((n_half, nb, n2), F32)],
        compiler_params=pltpu.CompilerParams(
            dimension_semantics=("arbitrary",), vmem_limit_bytes=_vmem_limit(est)),
        name="s5_scan_bwd" if reverse else "s5_scan_fwd",
    )(u, a, wb, wc)


def _ssm_params(a_re, a_im, log_dt, b_re, b_im, c_re, c_im, nb):
    g, p = a_re.shape
    h = b_re.shape[-1]
    n_half = (g * h) // V7X_MXU_DIM
    gh = g // n_half
    dt = jnp.exp(log_dt)[:, None]
    mag = jnp.exp(dt * a_re)
    ab_re = mag * jnp.cos(dt * a_im)
    ab_im = mag * jnp.sin(dt * a_im)
    den = a_re * a_re + a_im * a_im
    nr, ni = ab_re - 1.0, ab_im
    k_re = (nr * a_re + ni * a_im) / den
    k_im = (ni * a_re - nr * a_im) / den
    bb_re = k_re[..., None] * b_re - k_im[..., None] * b_im
    bb_im = k_re[..., None] * b_im + k_im[..., None] * b_re
    eye = jnp.eye(gh, dtype=F32)

    def in_proj(bb):
        bb = bb.reshape(n_half, gh, p, h)
        w = jnp.einsum('ngph,gk->nghkp', bb, eye)
        return w.reshape(n_half, gh * h, gh * p)

    def out_proj(c):
        c = c.reshape(n_half, gh, h, p)
        w = jnp.einsum('nghp,gk->ngpkh', c, eye)
        return w.reshape(n_half, gh * p, gh * h)

    wb = jnp.concatenate([in_proj(bb_re), in_proj(bb_im)], axis=-1).astype(BF16)
    wc = jnp.concatenate([out_proj(c_re), out_proj(-c_im)], axis=1).astype(BF16)
    a = jnp.concatenate([ab_re.reshape(n_half, gh * p), ab_im.reshape(n_half, gh * p)], axis=-1)
    a = jnp.broadcast_to(a[:, None, :], (n_half, nb, 2 * gh * p))
    return a, wb, wc


def _mix_out_kernel(x_ref, at_ref, yf_ref, yb_ref, u_ref, d_ref, wglu_ref, bglu_ref, gs_ref,
                    woa_ref, wos_ref, o_ref):
    y = yf_ref[0] + yb_ref[0] + d_ref[...] * u_ref[0]
    z = jax.nn.gelu(y)
    gate = jnp.dot(z.astype(BF16), wglu_ref[...], preferred_element_type=F32) + bglu_ref[...]
    s = z * jax.nn.sigmoid(gate)
    s_n = _rms(s, gs_ref[...]).astype(BF16)
    mixed = (jnp.dot(at_ref[0], woa_ref[...], preferred_element_type=F32)
             + jnp.dot(s_n, wos_ref[...], preferred_element_type=F32))
    o_ref[0] = x_ref[0] + mixed


def _mix_out(x3, attn_n, y_f, y_b, u, d, w_glu, b_glu, g_s, wo_a, wo_s, *, tm=512):
    b, l, dm = x3.shape
    d_attn = attn_n.shape[-1]
    d_ssm = w_glu.shape[0]
    xrow = pl.BlockSpec((1, tm, dm), lambda bi, i: (bi, i, 0))
    srow = pl.BlockSpec((1, tm, d_ssm), lambda bi, i: (bi, i, 0))
    est = 2 * (w_glu.size + wo_a.size + wo_s.size) + 4 * tm * dm * 4 + 12 * tm * d_ssm * 4
    return pl.pallas_call(
        _mix_out_kernel,
        out_shape=jax.ShapeDtypeStruct((b, l, dm), F32),
        grid=(b, l // tm),
        in_specs=[xrow,
                  pl.BlockSpec((1, tm, d_attn), lambda bi, i: (bi, i, 0)),
                  srow, srow, srow,
                  _const_spec((1, d_ssm)), _const_spec(w_glu.shape), _const_spec((1, d_ssm)),
                  _const_spec((1, d_ssm)), _const_spec(wo_a.shape), _const_spec(wo_s.shape)],
        out_specs=xrow,
        compiler_params=pltpu.CompilerParams(
            dimension_semantics=("arbitrary", "arbitrary"), vmem_limit_bytes=_vmem_limit(est)),
        name="mix_out",
    )(x3, attn_n, y_f, y_b, u, d, w_glu, b_glu, g_s, wo_a, wo_s)


def _ffn_weights(w_gate, w_up, w_down, fc=V7X_MXU_DIM):
    d, dff = w_gate.shape
    n = dff // fc
    wg = w_gate.astype(BF16).reshape(d, n, fc)
    wu = w_up.astype(BF16).reshape(d, n, fc)
    wgu = jnp.concatenate([wg, wu], axis=-1).transpose(1, 0, 2)
    wd = (0.5 * w_down).astype(BF16).reshape(n, fc, d)
    return wgu, wd


def kernel(x, rel_bias_table, ffn1_norm, ffn1_w_gate, ffn1_w_up, ffn1_w_down, mix_norm, w_in, attn_sink, ssm_a_re, ssm_a_im, ssm_log_dt, ssm_b_re, ssm_b_im, ssm_c_re, ssm_c_im, ssm_d, ssm_w_glu, ssm_b_glu, attn_out_norm, ssm_out_norm, w_out, ffn2_norm, ffn2_w_gate, ffn2_w_up, ffn2_w_down, final_norm):
    b, l, dm = x.shape
    depth = w_in.shape[0]
    d_attn = N_HEADS * HEAD_DIM
    d_kv2 = 2 * N_KV_HEADS * HEAD_DIM
    assert b == V7X_SUBLANES, "the scan keeps one batch sequence per sublane"

    bias = _rel_bias(rel_bias_table.astype(F32))
    row = lambda v: v.astype(F32).reshape(1, -1)

    for i in range(depth):
        x2 = _ffn(x.reshape(b * l, dm), row(ffn1_norm[i]),
                  *_ffn_weights(ffn1_w_gate[i], ffn1_w_up[i], ffn1_w_down[i]))
        x = x2.reshape(b, l, dm)

        q, kv, u = _proj(x, row(mix_norm[i]), w_in[i].astype(BF16), d_attn=d_attn, d_kv2=d_kv2)
        attn_n = _attention(q, kv, bias, attn_sink[i].astype(F32), row(attn_out_norm[i]))

        ys = []
        for di in range(2):
            a, wb, wc = _ssm_params(ssm_a_re[i, di], ssm_a_im[i, di], ssm_log_dt[i, di],
                                    ssm_b_re[i, di], ssm_b_im[i, di], ssm_c_re[i, di], ssm_c_im[i, di], b)
            ys.append(_ssm_scan(u, a, wb, wc, reverse=(di == 1)))

        wo = w_out[i].astype(BF16)
        x = _mix_out(x, attn_n, ys[0], ys[1], u, row(ssm_d[i]), ssm_w_glu[i].astype(BF16),
                     row(ssm_b_glu[i]), row(ssm_out_norm[i]), wo[:d_attn], wo[d_attn:])

        fin = row(final_norm) if i == depth - 1 else None
        x2 = _ffn(x.reshape(b * l, dm), row(ffn2_norm[i]),
                  *_ffn_weights(ffn2_w_gate[i], ffn2_w_up[i], ffn2_w_down[i]), fin)
        x = x2.reshape(b, l, dm)
    return x
```

```python
import functools
import math

import jax
import jax.numpy as jnp
from jax import lax
from jax.experimental import pallas as pl
from jax.experimental.pallas import tpu as pltpu

EPS = 1e-6
NEG_INF = -1e30

N_HEADS = 8
N_KV_HEADS = 2
HEAD_DIM = 64
SSM_GROUP = 16
SSM_STATE = 64
ATT_BLOCK = 128
N_BUCKETS = 32
MAX_DISTANCE = 128

V7X_SUBLANES = 8
V7X_LANES = 128
V7X_MXU_DIM = 256
V7X_VMEM_BYTES = 64 * 1024 * 1024

F32 = jnp.float32
BF16 = jnp.bfloat16


def _vmem_limit(nbytes):
    return int(min(nbytes + (8 << 20), V7X_VMEM_BYTES - (6 << 20)))


def _rms(x, g):
    return x * lax.rsqrt(jnp.mean(x * x, axis=-1, keepdims=True) + EPS) * g


def _const_spec(shape):
    nd = len(shape)
    return pl.BlockSpec(shape, lambda *_: (0,) * nd, pipeline_mode=pl.Buffered(1))


def _ffn_kernel(x_ref, g_ref, wgu_ref, wd_ref, *rest, n_chunks, final):
    if final:
        fg_ref, o_ref, h_scr, gu_scr = rest
    else:
        o_ref, h_scr, gu_scr = rest
    fc = wd_ref.shape[1]
    x = x_ref[...]
    h_scr[...] = _rms(x, g_ref[...]).astype(BF16)
    o_ref[...] = x

    def gate_up(c, slot):
        gu_scr[slot] = jnp.dot(h_scr[...], wgu_ref[c], preferred_element_type=F32)

    def down(c, slot):
        gu = gu_scr[slot]
        gate, up = gu[:, :fc], gu[:, fc:]
        act = (gate * jax.nn.sigmoid(gate) * up).astype(BF16)
        o_ref[...] += jnp.dot(act, wd_ref[c], preferred_element_type=F32)

    gate_up(0, 0)

    def two_chunks(k, carry):
        c = 2 * k
        gate_up(c + 1, 1)
        down(c, 0)
        gate_up(c + 2, 0)
        down(c + 1, 1)
        return carry

    assert n_chunks % 2 == 1
    lax.fori_loop(0, (n_chunks - 1) // 2, two_chunks, 0)
    down(n_chunks - 1, 0)
    if final:
        o_ref[...] = _rms(o_ref[...], fg_ref[...])


def _ffn(x2, g, wgu, wd, final_g=None, *, tm=512):
    t, d = x2.shape
    n_chunks, _, fc2 = wgu.shape
    final = final_g is not None
    row = pl.BlockSpec((tm, d), lambda i: (i, 0))
    in_specs = [row, _const_spec((1, d)), _const_spec(wgu.shape), _const_spec(wd.shape)]
    args = [x2, g, wgu, wd]
    if final:
        in_specs.append(_const_spec((1, d)))
        args.append(final_g)
    est = 2 * (wgu.size + wd.size) + 4 * tm * d * 4 + tm * d * 2 + 2 * tm * fc2 * 4
    return pl.pallas_call(
        functools.partial(_ffn_kernel, n_chunks=n_chunks, final=final),
        out_shape=jax.ShapeDtypeStruct((t, d), F32),
        grid=(t // tm,),
        in_specs=in_specs,
        out_specs=row,
        scratch_shapes=[pltpu.VMEM((tm, d), BF16), pltpu.VMEM((2, tm, fc2), F32)],
        compiler_params=pltpu.CompilerParams(
            dimension_semantics=("arbitrary",), vmem_limit_bytes=_vmem_limit(est)),
        name="ffn",
    )(*args)


def _proj_kernel(x_ref, g_ref, w_ref, q_ref, kv_ref, u_ref, *, d_attn, d_kv2):
    h = _rms(x_ref[0], g_ref[...]).astype(BF16)
    p = jnp.dot(h, w_ref[...], preferred_element_type=F32)
    q_ref[0] = (p[:, :d_attn] * (HEAD_DIM ** -0.5)).astype(BF16)
    kv_ref[0] = p[:, d_attn:d_attn + d_kv2].astype(BF16)
    u_ref[0] = p[:, d_attn + d_kv2:]


def _proj(x3, g, w_in, *, d_attn, d_kv2, tm=512):
    b, l, d = x3.shape
    d_in = w_in.shape[1]
    d_ssm = d_in - d_attn - d_kv2
    est = 2 * w_in.size + 2 * tm * d * 4 + 3 * tm * d_in * 4
    return pl.pallas_call(
        functools.partial(_proj_kernel, d_attn=d_attn, d_kv2=d_kv2),
        out_shape=(
            jax.ShapeDtypeStruct((b, l, d_attn), BF16),
            jax.ShapeDtypeStruct((b, l, d_kv2), BF16),
            jax.ShapeDtypeStruct((b, l, d_ssm), F32),
        ),
        grid=(b, l // tm),
        in_specs=[pl.BlockSpec((1, tm, d), lambda bi, i: (bi, i, 0)),
                  _const_spec((1, d)), _const_spec(w_in.shape)],
        out_specs=(
            pl.BlockSpec((1, tm, d_attn), lambda bi, i: (bi, i, 0)),
            pl.BlockSpec((1, tm, d_kv2), lambda bi, i: (bi, i, 0)),
            pl.BlockSpec((1, tm, d_ssm), lambda bi, i: (bi, i, 0)),
        ),
        compiler_params=pltpu.CompilerParams(
            dimension_semantics=("arbitrary", "arbitrary"), vmem_limit_bytes=_vmem_limit(est)),
        name="mix_proj",
    )(x3, g, w_in)


def _bias_kernel(table_ref, idx_ref, band_ref, o_ref):
    idx = idx_ref[...]
    band = band_ref[...] > 0
    for h in range(N_HEADS):
        acc = jnp.zeros(idx.shape, F32)
        for bkt in range(N_BUCKETS):
            acc = jnp.where(idx == bkt, table_ref[bkt, h], acc)
        o_ref[h] = jnp.where(band, acc, NEG_INF)


def _t5_bucket(rel):
    half = N_BUCKETS // 2
    max_exact = half // 2
    ret = jnp.where(rel > 0, half, 0)
    n = jnp.abs(rel)
    nf = jnp.maximum(n, 1).astype(F32)
    large = max_exact + (jnp.log(nf / max_exact) / math.log(MAX_DISTANCE / max_exact)
                         * (half - max_exact)).astype(jnp.int32)
    large = jnp.minimum(large, half - 1)
    return ret + jnp.where(n < max_exact, n, large)


def _rel_bias(table):
    rel = (jnp.arange(3 * ATT_BLOCK)[None, :] - ATT_BLOCK) - jnp.arange(ATT_BLOCK)[:, None]
    idx = _t5_bucket(rel).astype(jnp.int32)
    band = (jnp.abs(rel) <= ATT_BLOCK).astype(jnp.int32)
    return pl.pallas_call(
        _bias_kernel,
        out_shape=jax.ShapeDtypeStruct((N_HEADS, ATT_BLOCK, 3 * ATT_BLOCK), F32),
        in_specs=[pl.BlockSpec(memory_space=pltpu.SMEM),
                  pl.BlockSpec(memory_space=pltpu.VMEM),
                  pl.BlockSpec(memory_space=pltpu.VMEM)],
        out_specs=pl.BlockSpec(memory_space=pltpu.VMEM),
        name="rel_bias",
    )(table, idx, band)


def _attn_kernel(sink_ref, q_ref, kvp_ref, kvm_ref, kvn_ref, bias_ref, g_ref, o_ref,
                 kd_scr, ve_scr, vo_scr, *, nq):
    i = pl.program_id(1)
    n_i = pl.num_programs(1)
    blk = ATT_BLOCK
    dh = HEAD_DIM
    dk2 = N_KV_HEADS * dh
    grp = N_HEADS // N_KV_HEADS
    assert dk2 == V7X_LANES and grp == 4
    lane = lax.broadcasted_iota(jnp.int32, (1, dk2), 1)
    m_lo = (lane < dh).astype(BF16)
    m_hi = (lane >= dh).astype(BF16)

    kv = jnp.concatenate([kvp_ref[0], kvm_ref[0], kvn_ref[0]], axis=0)
    rows = kv.shape[0]
    k_row, v_row = kv[:, :dk2], kv[:, dk2:]
    k_swap = jnp.concatenate([k_row[:, dh:], k_row[:, :dh]], axis=-1)
    v_swap = jnp.concatenate([v_row[:, dh:], v_row[:, :dh]], axis=-1)
    ones_lo = jnp.broadcast_to(m_lo, (rows, dk2))
    ones_hi = jnp.broadcast_to(m_hi, (rows, dk2))
    kd_scr[0] = k_row * m_lo + k_swap * m_hi
    kd_scr[1] = k_swap * m_lo + k_row * m_hi
    ve_scr[0] = jnp.concatenate([v_row * m_lo, ones_lo], axis=-1)
    vo_scr[0] = jnp.concatenate([v_swap * m_hi, ones_hi], axis=-1)
    ve_scr[1] = jnp.concatenate([v_swap * m_lo, ones_lo], axis=-1)
    vo_scr[1] = jnp.concatenate([v_row * m_hi, ones_hi], axis=-1)

    col = lax.broadcasted_iota(jnp.int32, (1, 3 * blk), 1)
    edge_first = jnp.where(col < blk, jnp.where(i == 0, NEG_INF, 0.0), 0.0)
    edge_last = jnp.where(col >= 2 * blk, jnp.where(i == n_i - 1, NEG_INF, 0.0), 0.0)
    lane_lo = lane < dh

    for jb in range(nq):
        win = slice(jb * blk, (jb + 3) * blk)
        pieces = []
        for kh in range(N_KV_HEADS):
            qs = []
            for msk in (m_lo, m_hi):
                for pr in range(grp // 2):
                    c0 = kh * grp * dh + pr * dk2
                    qs.append(q_ref[0, jb * blk:(jb + 1) * blk, c0:c0 + dk2] * msk)
            q_st = jnp.concatenate(qs, axis=0)
            s = lax.dot_general(q_st, kd_scr[kh, win, :], (((1,), (1,)), ((), ())),
                                preferred_element_type=F32)
            s = s + bias_ref[kh]
            if jb == 0:
                s = s + edge_first
            if jb == nq - 1:
                s = s + edge_last
            s_max = jnp.maximum(jnp.maximum(s[:, :blk], s[:, blk:2 * blk]), s[:, 2 * blk:])
            p, t = [], []
            for r, hh in enumerate((0, 2, 1, 3)):
                rs = slice(r * blk, (r + 1) * blk)
                sink = sink_ref[kh * grp + hh]
                m = jnp.maximum(jnp.max(s_max[rs], axis=-1, keepdims=True), sink)
                p.append(jnp.exp(s[rs] - m).astype(BF16))
                t.append(jnp.exp(sink - m))
            for pr in range(grp // 2):
                ev, od = pr, grp // 2 + pr
                nd = (jnp.dot(p[ev], ve_scr[kh, win, :], preferred_element_type=F32)
                      + jnp.dot(p[od], vo_scr[kh, win, :], preferred_element_type=F32))
                den = nd[:, dk2:] + jnp.where(lane_lo, t[ev], t[od])
                pieces.append(nd[:, :dk2] / den)
        o = jnp.concatenate(pieces, axis=-1)
        o_ref[0, jb * blk:(jb + 1) * blk, :] = _rms(o, g_ref[...]).astype(BF16)


def _attention(q, kv, bias, sink, g, *, nq=4):
    b, l, d_attn = q.shape
    d_kv2 = kv.shape[-1]
    blk = ATT_BLOCK
    tq = nq * blk
    nb = l // blk
    grp = N_HEADS // N_KV_HEADS
    bias_g = bias.reshape(N_KV_HEADS, grp, blk, 3 * blk)[:, jnp.array([0, 2, 1, 3])]
    bias_g = bias_g.reshape(N_KV_HEADS, grp * blk, 3 * blk)
    rows = tq + 2 * blk
    est = (bias.size * 4 + 6 * tq * (d_attn + d_kv2) * 2 + N_KV_HEADS * rows * 5 * V7X_LANES * 2
           + 6 * grp * blk * 3 * blk * 4)
    return pl.pallas_call(
        functools.partial(_attn_kernel, nq=nq),
        out_shape=jax.ShapeDtypeStruct((b, l, d_attn), BF16),
        grid=(b, l // tq),
        in_specs=[
            pl.BlockSpec(memory_space=pltpu.SMEM),
            pl.BlockSpec((1, tq, d_attn), lambda bi, i: (bi, i, 0)),
            pl.BlockSpec((1, blk, d_kv2), lambda bi, i: (bi, jnp.maximum(i * nq - 1, 0), 0)),
            pl.BlockSpec((1, tq, d_kv2), lambda bi, i: (bi, i, 0)),
            pl.BlockSpec((1, blk, d_kv2), lambda bi, i: (bi, jnp.minimum((i + 1) * nq, nb - 1), 0)),
            _const_spec(bias_g.shape),
            _const_spec((1, d_attn)),
        ],
        out_specs=pl.BlockSpec((1, tq, d_attn), lambda bi, i: (bi, i, 0)),
        scratch_shapes=[pltpu.VMEM((N_KV_HEADS, rows, V7X_LANES), BF16),
                        pltpu.VMEM((N_KV_HEADS, rows, 2 * V7X_LANES), BF16),
                        pltpu.VMEM((N_KV_HEADS, rows, 2 * V7X_LANES), BF16)],
        compiler_params=pltpu.CompilerParams(
            dimension_semantics=("arbitrary", "arbitrary"), vmem_limit_bytes=_vmem_limit(est)),
        name="band_attn",
    )(sink, q, kv, kv, kv, bias_g, g)


def _ssm_kernel(u_ref, a_ref, wb_ref, wc_ref, y_ref, ut_scr, yt_scr, x_scr, s_scr, st_scr,
                *, tc, nb, reverse, lane_tile):
    n_half, _, n2 = a_ref.shape
    nre = n2 // 2
    kh = wb_ref.shape[1]
    rows2 = 2 * nb
    n_pairs = tc // 2
    n_slab = ut_scr.shape[0]
    slab_per_half = kh // V7X_LANES

    @pl.when(pl.program_id(0) == 0)
    def _():
        st_scr[...] = jnp.zeros_like(st_scr)

    for bi in range(nb):
        for sl in range(n_slab):
            ut_scr.at[sl][pl.ds(bi, tc, stride=nb), :] = u_ref[bi, :, sl * V7X_LANES:(sl + 1) * V7X_LANES]

    for hf in range(n_half):
        u_h = jnp.concatenate([ut_scr[hf * slab_per_half + k] for k in range(slab_per_half)], axis=-1)
        x_scr[...] = jnp.dot(u_h.astype(BF16), wb_ref[hf], preferred_element_type=F32)
        for lt in range(nre // lane_tile):
            re_sl = slice(lt * lane_tile, (lt + 1) * lane_tile)
            im_sl = slice(nre + lt * lane_tile, nre + (lt + 1) * lane_tile)
            a_re = a_ref[hf, :, re_sl]
            a_im = a_ref[hf, :, im_sl]

            def step(s_re, s_im, x_re, x_im):
                n_re = a_re * s_re - a_im * s_im + x_re
                n_im = a_re * s_im + a_im * s_re + x_im
                return n_re, n_im

            def pair(j, carry):
                s_re, s_im = carry
                jj = (n_pairs - 1 - j) if reverse else j
                r0 = pl.multiple_of(jj * rows2, rows2)
                x_re = x_scr[pl.ds(r0, rows2), re_sl]
                x_im = x_scr[pl.ds(r0, rows2), im_sl]
                first, second = (slice(nb, rows2), slice(0, nb)) if reverse else (slice(0, nb), slice(nb, rows2))
                s1_re, s1_im = step(s_re, s_im, x_re[first], x_im[first])
                s2_re, s2_im = step(s1_re, s1_im, x_re[second], x_im[second])
                lo_re, hi_re = (s2_re, s1_re) if reverse else (s1_re, s2_re)
                lo_im, hi_im = (s2_im, s1_im) if reverse else (s1_im, s2_im)
                s_scr[pl.ds(r0, rows2), re_sl] = jnp.concatenate([lo_re, hi_re], axis=0).astype(BF16)
                s_scr[pl.ds(r0, rows2), im_sl] = jnp.concatenate([lo_im, hi_im], axis=0).astype(BF16)
                return s2_re, s2_im

            init = (st_scr[hf, :, re_sl], st_scr[hf, :, im_sl])
            f_re, f_im = lax.fori_loop(0, n_pairs, pair, init, unroll=2)
            st_scr[hf, :, re_sl] = f_re
            st_scr[hf, :, im_sl] = f_im
        y_h = jnp.dot(s_scr[...], wc_ref[hf], preferred_element_type=F32)
        for k in range(slab_per_half):
            yt_scr[hf * slab_per_half + k] = y_h[:, k * V7X_LANES:(k + 1) * V7X_LANES]

    for bi in range(nb):
        for sl in range(n_slab):
            y_ref[bi, :, sl * V7X_LANES:(sl + 1) * V7X_LANES] = yt_scr.at[sl][pl.ds(bi, tc, stride=nb), :]


def _ssm_scan(u, a, wb, wc, *, reverse, tc=64, lane_tile=512):
    nb, l, d_ssm = u.shape
    n_half, kh, n2 = wb.shape
    m = tc * nb
    n_t = l // tc
    idx = (lambda i: (0, n_t - 1 - i, 0)) if reverse else (lambda i: (0, i, 0))
    est = 6 * m * d_ssm * 4 + m * n2 * 6 + 2 * (wb.size + wc.size) + 3 * a.size * 4
    return pl.pallas_call(
        functools.partial(_ssm_kernel, tc=tc, nb=nb, reverse=reverse, lane_tile=lane_tile),
        out_shape=jax.ShapeDtypeStruct((nb, l, d_ssm), F32),
        grid=(n_t,),
        in_specs=[pl.BlockSpec((nb, tc, d_ssm), idx),
                  _const_spec(a.shape), _const_spec(wb.shape), _const_spec(wc.shape)],
        out_specs=pl.BlockSpec((nb, tc, d_ssm), idx),
        scratch_shapes=[pltpu.VMEM((d_ssm // V7X_LANES, m, V7X_LANES), F32),
                        pltpu.VMEM((d_ssm // V7X_LANES, m, V7X_LANES), F32),
                        pltpu.VMEM((m, n2), F32),
                        pltpu.VMEM((m, n2), BF16),
                        pltpu.VMEM((n_half, nb, n2), F32)],
        compiler_params=pltpu.CompilerParams(
            dimension_semantics=("arbitrary",), vmem_limit_bytes=_vmem_limit(est)),
        name="s5_scan_bwd" if reverse else "s5_scan_fwd",
    )(u, a, wb, wc)


def _ssm_params(a_re, a_im, log_dt, b_re, b_im, c_re, c_im, nb):
    g, p = a_re.shape
    h = b_re.shape[-1]
    n_half = (g * h) // V7X_MXU_DIM
    gh = g // n_half
    dt = jnp.exp(log_dt)[:, None]
    mag = jnp.exp(dt * a_re)
    ab_re = mag * jnp.cos(dt * a_im)
    ab_im = mag * jnp.sin(dt * a_im)
    den = a_re * a_re + a_im * a_im
    nr, ni = ab_re - 1.0, ab_im
    k_re = (nr * a_re + ni * a_im) / den
    k_im = (ni * a_re - nr * a_im) / den
    bb_re = k_re[..., None] * b_re - k_im[..., None] * b_im
    bb_im = k_re[..., None] * b_im + k_im[..., None] * b_re
    eye = jnp.eye(gh, dtype=F32)

    def in_proj(bb):
        bb = bb.reshape(n_half, gh, p, h)
        w = jnp.einsum('ngph,gk->nghkp', bb, eye)
        return w.reshape(n_half, gh * h, gh * p)

    def out_proj(c):
        c = c.reshape(n_half, gh, h, p)
        w = jnp.einsum('nghp,gk->ngpkh', c, eye)
        return w.reshape(n_half, gh * p, gh * h)

    wb = jnp.concatenate([in_proj(bb_re), in_proj(bb_im)], axis=-1).astype(BF16)
    wc = jnp.concatenate([out_proj(c_re), out_proj(-c_im)], axis=1).astype(BF16)
    a = jnp.concatenate([ab_re.reshape(n_half, gh * p), ab_im.reshape(n_half, gh * p)], axis=-1)
    a = jnp.broadcast_to(a[:, None, :], (n_half, nb, 2 * gh * p))
    return a, wb, wc


def _mix_out_kernel(x_ref, at_ref, yf_ref, yb_ref, u_ref, d_ref, wglu_ref, bglu_ref, gs_ref,
                    woa_ref, wos_ref, o_ref):
    y = yf_ref[0] + yb_ref[0] + d_ref[...] * u_ref[0]
    z = jax.nn.gelu(y)
    gate = jnp.dot(z.astype(BF16), wglu_ref[...], preferred_element_type=F32) + bglu_ref[...]
    s = z * jax.nn.sigmoid(gate)
    s_n = _rms(s, gs_ref[...]).astype(BF16)
    mixed = (jnp.dot(at_ref[0], woa_ref[...], preferred_element_type=F32)
             + jnp.dot(s_n, wos_ref[...], preferred_element_type=F32))
    o_ref[0] = x_ref[0] + mixed


def _mix_out(x3, attn_n, y_f, y_b, u, d, w_glu, b_glu, g_s, wo_a, wo_s, *, tm=512):
    b, l, dm = x3.shape
    d_attn = attn_n.shape[-1]
    d_ssm = w_glu.shape[0]
    xrow = pl.BlockSpec((1, tm, dm), lambda bi, i: (bi, i, 0))
    srow = pl.BlockSpec((1, tm, d_ssm), lambda bi, i: (bi, i, 0))
    est = 2 * (w_glu.size + wo_a.size + wo_s.size) + 4 * tm * dm * 4 + 12 * tm * d_ssm * 4
    return pl.pallas_call(
        _mix_out_kernel,
        out_shape=jax.ShapeDtypeStruct((b, l, dm), F32),
        grid=(b, l // tm),
        in_specs=[xrow,
                  pl.BlockSpec((1, tm, d_attn), lambda bi, i: (bi, i, 0)),
                  srow, srow, srow,
                  _const_spec((1, d_ssm)), _const_spec(w_glu.shape), _const_spec((1, d_ssm)),
                  _const_spec((1, d_ssm)), _const_spec(wo_a.shape), _const_spec(wo_s.shape)],
        out_specs=xrow,
        compiler_params=pltpu.CompilerParams(
            dimension_semantics=("arbitrary", "arbitrary"), vmem_limit_bytes=_vmem_limit(est)),
        name="mix_out",
    )(x3, attn_n, y_f, y_b, u, d, w_glu, b_glu, g_s, wo_a, wo_s)


def _ffn_weights(w_gate, w_up, w_down, fc=V7X_MXU_DIM):
    d, dff = w_gate.shape
    n = dff // fc
    wg = w_gate.astype(BF16).reshape(d, n, fc)
    wu = w_up.astype(BF16).reshape(d, n, fc)
    wgu = jnp.concatenate([wg, wu], axis=-1).transpose(1, 0, 2)
    wd = (0.5 * w_down).astype(BF16).reshape(n, fc, d)
    return wgu, wd


def kernel(x, rel_bias_table, ffn1_norm, ffn1_w_gate, ffn1_w_up, ffn1_w_down, mix_norm, w_in, attn_sink, ssm_a_re, ssm_a_im, ssm_log_dt, ssm_b_re, ssm_b_im, ssm_c_re, ssm_c_im, ssm_d, ssm_w_glu, ssm_b_glu, attn_out_norm, ssm_out_norm, w_out, ffn2_norm, ffn2_w_gate, ffn2_w_up, ffn2_w_down, final_norm):
    b, l, dm = x.shape
    depth = w_in.shape[0]
    d_attn = N_HEADS * HEAD_DIM
    d_kv2 = 2 * N_KV_HEADS * HEAD_DIM
    assert b == V7X_SUBLANES, "the scan keeps one batch sequence per sublane"

    bias = _rel_bias(rel_bias_table.astype(F32))
    row = lambda v: v.astype(F32).reshape(1, -1)

    for i in range(depth):
        x2 = _ffn(x.reshape(b * l, dm), row(ffn1_norm[i]),
                  *_ffn_weights(ffn1_w_gate[i], ffn1_w_up[i], ffn1_w_down[i]))
        x = x2.reshape(b, l, dm)

        q, kv, u = _proj(x, row(mix_norm[i]), w_in[i].astype(BF16), d_attn=d_attn, d_kv2=d_kv2)
        attn_n = _attention(q, kv, bias, attn_sink[i].astype(F32), row(attn_out_norm[i]))

        ys = []
        for di in range(2):
            a, wb, wc = _ssm_params(ssm_a_re[i, di], ssm_a_im[i, di], ssm_log_dt[i, di],
                                    ssm_b_re[i, di], ssm_b_im[i, di], ssm_c_re[i, di], ssm_c_im[i, di], b)
            ys.append(_ssm_scan(u, a, wb, wc, reverse=(di == 1)))

        wo = w_out[i].astype(BF16)
        x = _mix_out(x, attn_n, ys[0], ys[1], u, row(ssm_d[i]), ssm_w_glu[i].astype(BF16),
                     row(ssm_b_glu[i]), row(ssm_out_norm[i]), wo[:d_attn], wo[d_attn:])

        fin = row(final_norm) if i == depth - 1 else None
        x2 = _ffn(x.reshape(b * l, dm), row(ffn2_norm[i]),
                  *_ffn_weights(ffn2_w_gate[i], ffn2_w_up[i], ffn2_w_down[i]), fin)
        x = x2.reshape(b, l, dm)
    return x
```

```python
import functools
import math

import jax
import jax.numpy as jnp
from jax import lax
from jax.experimental import pallas as pl
from jax.experimental.pallas import tpu as pltpu

EPS = 1e-6
NEG_INF = -1e30

N_HEADS = 8
N_KV_HEADS = 2
HEAD_DIM = 64
SSM_GROUP = 16
SSM_STATE = 64
ATT_BLOCK = 128
N_BUCKETS = 32
MAX_DISTANCE = 128

V7X_SUBLANES = 8
V7X_LANES = 128
V7X_MXU_DIM = 256
V7X_VMEM_BYTES = 64 * 1024 * 1024

F32 = jnp.float32
BF16 = jnp.bfloat16


def _vmem_limit(nbytes):
    return int(min(nbytes + (8 << 20), V7X_VMEM_BYTES - (6 << 20)))


def _rms(x, g):
    return x * lax.rsqrt(jnp.mean(x * x, axis=-1, keepdims=True) + EPS) * g


def _const_spec(shape):
    nd = len(shape)
    return pl.BlockSpec(shape, lambda *_: (0,) * nd, pipeline_mode=pl.Buffered(1))


def _ffn_kernel(x_ref, g_ref, wgu_ref, wd_ref, *rest, n_chunks, final):
    if final:
        fg_ref, o_ref, h_scr, gu_scr = rest
    else:
        o_ref, h_scr, gu_scr = rest
    fc = wd_ref.shape[1]
    x = x_ref[...]
    h_scr[...] = _rms(x, g_ref[...]).astype(BF16)
    o_ref[...] = x

    def gate_up(c, slot):
        gu_scr[slot] = jnp.dot(h_scr[...], wgu_ref[c], preferred_element_type=F32)

    def down(c, slot):
        gu = gu_scr[slot]
        gate, up = gu[:, :fc], gu[:, fc:]
        act = (gate * jax.nn.sigmoid(gate) * up).astype(BF16)
        o_ref[...] += jnp.dot(act, wd_ref[c], preferred_element_type=F32)

    gate_up(0, 0)

    def two_chunks(k, carry):
        c = 2 * k
        gate_up(c + 1, 1)
        down(c, 0)
        gate_up(c + 2, 0)
        down(c + 1, 1)
        return carry

    assert n_chunks % 2 == 1
    lax.fori_loop(0, (n_chunks - 1) // 2, two_chunks, 0)
    down(n_chunks - 1, 0)
    if final:
        o_ref[...] = _rms(o_ref[...], fg_ref[...])


def _ffn(x2, g, wgu, wd, final_g=None, *, tm=512):
    t, d = x2.shape
    n_chunks, _, fc2 = wgu.shape
    final = final_g is not None
    row = pl.BlockSpec((tm, d), lambda i: (i, 0))
    in_specs = [row, _const_spec((1, d)), _const_spec(wgu.shape), _const_spec(wd.shape)]
    args = [x2, g, wgu, wd]
    if final:
        in_specs.append(_const_spec((1, d)))
        args.append(final_g)
    est = 2 * (wgu.size + wd.size) + 4 * tm * d * 4 + tm * d * 2 + 2 * tm * fc2 * 4
    return pl.pallas_call(
        functools.partial(_ffn_kernel, n_chunks=n_chunks, final=final),
        out_shape=jax.ShapeDtypeStruct((t, d), F32),
        grid=(t // tm,),
        in_specs=in_specs,
        out_specs=row,
        scratch_shapes=[pltpu.VMEM((tm, d), BF16), pltpu.VMEM((2, tm, fc2), F32)],
        compiler_params=pltpu.CompilerParams(
            dimension_semantics=("arbitrary",), vmem_limit_bytes=_vmem_limit(est)),
        name="ffn",
    )(*args)


def _proj_kernel(x_ref, g_ref, w_ref, q_ref, kv_ref, u_ref, *, d_attn, d_kv2):
    h = _rms(x_ref[0], g_ref[...]).astype(BF16)
    p = jnp.dot(h, w_ref[...], preferred_element_type=F32)
    q_ref[0] = (p[:, :d_attn] * (HEAD_DIM ** -0.5)).astype(BF16)
    kv_ref[0] = p[:, d_attn:d_attn + d_kv2].astype(BF16)
    u_ref[0] = p[:, d_attn + d_kv2:]


def _proj(x3, g, w_in, *, d_attn, d_kv2, tm=512):
    b, l, d = x3.shape
    d_in = w_in.shape[1]
    d_ssm = d_in - d_attn - d_kv2
    est = 2 * w_in.size + 2 * tm * d * 4 + 3 * tm * d_in * 4
    return pl.pallas_call(
        functools.partial(_proj_kernel, d_attn=d_attn, d_kv2=d_kv2),
        out_shape=(
            jax.ShapeDtypeStruct((b, l, d_attn), BF16),
            jax.ShapeDtypeStruct((b, l, d_kv2), BF16),
            jax.ShapeDtypeStruct((b, l, d_ssm), F32),
        ),
        grid=(b, l // tm),
        in_specs=[pl.BlockSpec((1, tm, d), lambda bi, i: (bi, i, 0)),
                  _const_spec((1, d)), _const_spec(w_in.shape)],
        out_specs=(
            pl.BlockSpec((1, tm, d_attn), lambda bi, i: (bi, i, 0)),
            pl.BlockSpec((1, tm, d_kv2), lambda bi, i: (bi, i, 0)),
            pl.BlockSpec((1, tm, d_ssm), lambda bi, i: (bi, i, 0)),
        ),
        compiler_params=pltpu.CompilerParams(
            dimension_semantics=("arbitrary", "arbitrary"), vmem_limit_bytes=_vmem_limit(est)),
        name="mix_proj",
    )(x3, g, w_in)


def _bias_kernel(table_ref, idx_ref, band_ref, o_ref):
    idx = idx_ref[...]
    band = band_ref[...] > 0
    for h in range(N_HEADS):
        acc = jnp.zeros(idx.shape, F32)
        for bkt in range(N_BUCKETS):
            acc = jnp.where(idx == bkt, table_ref[bkt, h], acc)
        o_ref[h] = jnp.where(band, acc, NEG_INF)


def _t5_bucket(rel):
    half = N_BUCKETS // 2
    max_exact = half // 2
    ret = jnp.where(rel > 0, half, 0)
    n = jnp.abs(rel)
    nf = jnp.maximum(n, 1).astype(F32)
    large = max_exact + (jnp.log(nf / max_exact) / math.log(MAX_DISTANCE / max_exact)
                         * (half - max_exact)).astype(jnp.int32)
    large = jnp.minimum(large, half - 1)
    return ret + jnp.where(n < max_exact, n, large)


def _rel_bias(table):
    rel = (jnp.arange(3 * ATT_BLOCK)[None, :] - ATT_BLOCK) - jnp.arange(ATT_BLOCK)[:, None]
    idx = _t5_bucket(rel).astype(jnp.int32)
    band = (jnp.abs(rel) <= ATT_BLOCK).astype(jnp.int32)
    return pl.pallas_call(
        _bias_kernel,
        out_shape=jax.ShapeDtypeStruct((N_HEADS, ATT_BLOCK, 3 * ATT_BLOCK), F32),
        in_specs=[pl.BlockSpec(memory_space=pltpu.SMEM),
                  pl.BlockSpec(memory_space=pltpu.VMEM),
                  pl.BlockSpec(memory_space=pltpu.VMEM)],
        out_specs=pl.BlockSpec(memory_space=pltpu.VMEM),
        name="rel_bias",
    )(table, idx, band)


def _attn_kernel(sink_ref, q_ref, kvp_ref, kvm_ref, kvn_ref, bias_ref, g_ref, o_ref,
                 kd_scr, ve_scr, vo_scr, *, nq):
    i = pl.program_id(1)
    n_i = pl.num_programs(1)
    blk = ATT_BLOCK
    dh = HEAD_DIM
    dk2 = N_KV_HEADS * dh
    grp = N_HEADS // N_KV_HEADS
    assert dk2 == V7X_LANES and grp == 4
    lane = lax.broadcasted_iota(jnp.int32, (1, dk2), 1)
    m_lo = (lane < dh).astype(BF16)
    m_hi = (lane >= dh).astype(BF16)

    kv = jnp.concatenate([kvp_ref[0], kvm_ref[0], kvn_ref[0]], axis=0)
    rows = kv.shape[0]
    k_row, v_row = kv[:, :dk2], kv[:, dk2:]
    k_swap = jnp.concatenate([k_row[:, dh:], k_row[:, :dh]], axis=-1)
    v_swap = jnp.concatenate([v_row[:, dh:], v_row[:, :dh]], axis=-1)
    ones_lo = jnp.broadcast_to(m_lo, (rows, dk2))
    ones_hi = jnp.broadcast_to(m_hi, (rows, dk2))
    kd_scr[0] = k_row * m_lo + k_swap * m_hi
    kd_scr[1] = k_swap * m_lo + k_row * m_hi
    ve_scr[0] = jnp.concatenate([v_row * m_lo, ones_lo], axis=-1)
    vo_scr[0] = jnp.concatenate([v_swap * m_hi, ones_hi], axis=-1)
    ve_scr[1] = jnp.concatenate([v_swap * m_lo, ones_lo], axis=-1)
    vo_scr[1] = jnp.concatenate([v_row * m_hi, ones_hi], axis=-1)

    col = lax.broadcasted_iota(jnp.int32, (1, 3 * blk), 1)
    edge_first = jnp.where(col < blk, jnp.where(i == 0, NEG_INF, 0.0), 0.0)
    edge_last = jnp.where(col >= 2 * blk, jnp.where(i == n_i - 1, NEG_INF, 0.0), 0.0)
    lane_lo = lane < dh

    for jb in range(nq):
        win = slice(jb * blk, (jb + 3) * blk)
        pieces = []
        for kh in range(N_KV_HEADS):
            qs = []
            for msk in (m_lo, m_hi):
                for pr in range(grp // 2):
                    c0 = kh * grp * dh + pr * dk2
                    qs.append(q_ref[0, jb * blk:(jb + 1) * blk, c0:c0 + dk2] * msk)
            q_st = jnp.concatenate(qs, axis=0)
            s = lax.dot_general(q_st, kd_scr[kh, win, :], (((1,), (1,)), ((), ())),
                                preferred_element_type=F32)
            s = s + bias_ref[kh]
            if jb == 0:
                s = s + edge_first
            if jb == nq - 1:
                s = s + edge_last
            s_max = jnp.maximum(jnp.maximum(s[:, :blk], s[:, blk:2 * blk]), s[:, 2 * blk:])
            p, t = [], []
            for r, hh in enumerate((0, 2, 1, 3)):
                rs = slice(r * blk, (r + 1) * blk)
                sink = sink_ref[kh * grp + hh]
                m = jnp.maximum(jnp.max(s_max[rs], axis=-1, keepdims=True), sink)
                p.append(jnp.exp(s[rs] - m).astype(BF16))
                t.append(jnp.exp(sink - m))
            for pr in range(grp // 2):
                ev, od = pr, grp // 2 + pr
                nd = (jnp.dot(p[ev], ve_scr[kh, win, :], preferred_element_type=F32)
                      + jnp.dot(p[od], vo_scr[kh, win, :], preferred_element_type=F32))
                den = nd[:, dk2:] + jnp.where(lane_lo, t[ev], t[od])
                pieces.append(nd[:, :dk2] / den)
        o = jnp.concatenate(pieces, axis=-1)
        o_ref[0, jb * blk:(jb + 1) * blk, :] = _rms(o, g_ref[...]).astype(BF16)


def _attention(q, kv, bias, sink, g, *, nq=4):
    b, l, d_attn = q.shape
    d_kv2 = kv.shape[-1]
    blk = ATT_BLOCK
    tq = nq * blk
    nb = l // blk
    grp = N_HEADS // N_KV_HEADS
    bias_g = bias.reshape(N_KV_HEADS, grp, blk, 3 * blk)[:, jnp.array([0, 2, 1, 3])]
    bias_g = bias_g.reshape(N_KV_HEADS, grp * blk, 3 * blk)
    rows = tq + 2 * blk
    est = (bias.size * 4 + 6 * tq * (d_attn + d_kv2) * 2 + N_KV_HEADS * rows * 5 * V7X_LANES * 2
           + 6 * grp * blk * 3 * blk * 4)
    return pl.pallas_call(
        functools.partial(_attn_kernel, nq=nq),
        out_shape=jax.ShapeDtypeStruct((b, l, d_attn), BF16),
        grid=(b, l // tq),
        in_specs=[
            pl.BlockSpec(memory_space=pltpu.SMEM),
            pl.BlockSpec((1, tq, d_attn), lambda bi, i: (bi, i, 0)),
            pl.BlockSpec((1, blk, d_kv2), lambda bi, i: (bi, jnp.maximum(i * nq - 1, 0), 0)),
            pl.BlockSpec((1, tq, d_kv2), lambda bi, i: (bi, i, 0)),
            pl.BlockSpec((1, blk, d_kv2), lambda bi, i: (bi, jnp.minimum((i + 1) * nq, nb - 1), 0)),
            _const_spec(bias_g.shape),
            _const_spec((1, d_attn)),
        ],
        out_specs=pl.BlockSpec((1, tq, d_attn), lambda bi, i: (bi, i, 0)),
        scratch_shapes=[pltpu.VMEM((N_KV_HEADS, rows, V7X_LANES), BF16),
                        pltpu.VMEM((N_KV_HEADS, rows, 2 * V7X_LANES), BF16),
                        pltpu.VMEM((N_KV_HEADS, rows, 2 * V7X_LANES), BF16)],
        compiler_params=pltpu.CompilerParams(
            dimension_semantics=("arbitrary", "arbitrary"), vmem_limit_bytes=_vmem_limit(est)),
        name="band_attn",
    )(sink, q, kv, kv, kv, bias_g, g)


def _ssm_kernel(u_ref, a_ref, wb_ref, wc_ref, y_ref, ut_scr, yt_scr, x_scr, s_scr, st_scr,
                *, tc, nb, reverse, lane_tile):
    n_half, _, n2 = a_ref.shape
    nre = n2 // 2
    kh = wb_ref.shape[1]
    rows2 = 2 * nb
    n_pairs = tc // 2
    n_slab = ut_scr.shape[0]
    slab_per_half = kh // V7X_LANES

    @pl.when(pl.program_id(0) == 0)
    def _():
        st_scr[...] = jnp.zeros_like(st_scr)

    for bi in range(nb):
        for sl in range(n_slab):
            ut_scr.at[sl][pl.ds(bi, tc, stride=nb), :] = u_ref[bi, :, sl * V7X_LANES:(sl + 1) * V7X_LANES]

    for hf in range(n_half):
        u_h = jnp.concatenate([ut_scr[hf * slab_per_half + k] for k in range(slab_per_half)], axis=-1)
        x_scr[hf] = jnp.dot(u_h.astype(BF16), wb_ref[hf], preferred_element_type=F32)
    for hf in range(n_half):
        for lt in range(nre // lane_tile):
            re_sl = slice(lt * lane_tile, (lt + 1) * lane_tile)
            im_sl = slice(nre + lt * lane_tile, nre + (lt + 1) * lane_tile)
            a_re = a_ref[hf, :, re_sl]
            a_im = a_ref[hf, :, im_sl]

            def step(s_re, s_im, x_re, x_im):
                n_re = a_re * s_re - a_im * s_im + x_re
                n_im = a_re * s_im + a_im * s_re + x_im
                return n_re, n_im

            s_re, s_im = st_scr[hf, :, re_sl], st_scr[hf, :, im_sl]
            for j in range(n_pairs):
                r0 = ((n_pairs - 1 - j) if reverse else j) * rows2
                x_re = x_scr[hf, r0:r0 + rows2, re_sl]
                x_im = x_scr[hf, r0:r0 + rows2, im_sl]
                first, second = (slice(nb, rows2), slice(0, nb)) if reverse else (slice(0, nb), slice(nb, rows2))
                s1_re, s1_im = step(s_re, s_im, x_re[first], x_im[first])
                s_re, s_im = step(s1_re, s1_im, x_re[second], x_im[second])
                lo_re, hi_re = (s_re, s1_re) if reverse else (s1_re, s_re)
                lo_im, hi_im = (s_im, s1_im) if reverse else (s1_im, s_im)
                s_scr[hf, r0:r0 + rows2, re_sl] = jnp.concatenate([lo_re, hi_re], axis=0).astype(BF16)
                s_scr[hf, r0:r0 + rows2, im_sl] = jnp.concatenate([lo_im, hi_im], axis=0).astype(BF16)
            st_scr[hf, :, re_sl] = s_re
            st_scr[hf, :, im_sl] = s_im
        mh = (tc * nb) // 2
        for r in range(2):
            y_h = jnp.dot(s_scr[hf, r * mh:(r + 1) * mh, :], wc_ref[hf], preferred_element_type=F32)
            for k in range(slab_per_half):
                yt_scr[hf * slab_per_half + k, r * mh:(r + 1) * mh, :] = y_h[:, k * V7X_LANES:(k + 1) * V7X_LANES]

    for bi in range(nb):
        for sl in range(n_slab):
            y_ref[bi, :, sl * V7X_LANES:(sl + 1) * V7X_LANES] = yt_scr.at[sl][pl.ds(bi, tc, stride=nb), :]


def _ssm_scan(u, a, wb, wc, *, reverse, tc=64, lane_tile=512):
    nb, l, d_ssm = u.shape
    n_half, kh, n2 = wb.shape
    m = tc * nb
    n_t = l // tc
    idx = (lambda i: (0, n_t - 1 - i, 0)) if reverse else (lambda i: (0, i, 0))
    est = 6 * m * d_ssm * 4 + m * n2 * 6 + 2 * (wb.size + wc.size) + 3 * a.size * 4
    return pl.pallas_call(
        functools.partial(_ssm_kernel, tc=tc, nb=nb, reverse=reverse, lane_tile=lane_tile),
        out_shape=jax.ShapeDtypeStruct((nb, l, d_ssm), F32),
        grid=(n_t,),
        in_specs=[pl.BlockSpec((nb, tc, d_ssm), idx),
                  _const_spec(a.shape), _const_spec(wb.shape), _const_spec(wc.shape)],
        out_specs=pl.BlockSpec((nb, tc, d_ssm), idx),
        scratch_shapes=[pltpu.VMEM((d_ssm // V7X_LANES, m, V7X_LANES), F32),
                        pltpu.VMEM((d_ssm // V7X_LANES, m, V7X_LANES), F32),
                        pltpu.VMEM((n_half, m, n2), F32),
                        pltpu.VMEM((n_half, m, n2), BF16),
                        pltpu.VMEM((n_half, nb, n2), F32)],
        compiler_params=pltpu.CompilerParams(
            dimension_semantics=("arbitrary",), vmem_limit_bytes=_vmem_limit(est)),
        name="s5_scan_bwd" if reverse else "s5_scan_fwd",
    )(u, a, wb, wc)


def _ssm_params(a_re, a_im, log_dt, b_re, b_im, c_re, c_im, nb):
    g, p = a_re.shape
    h = b_re.shape[-1]
    n_half = (g * h) // V7X_MXU_DIM
    gh = g // n_half
    dt = jnp.exp(log_dt)[:, None]
    mag = jnp.exp(dt * a_re)
    ab_re = mag * jnp.cos(dt * a_im)
    ab_im = mag * jnp.sin(dt * a_im)
    den = a_re * a_re + a_im * a_im
    nr, ni = ab_re - 1.0, ab_im
    k_re = (nr * a_re + ni * a_im) / den
    k_im = (ni * a_re - nr * a_im) / den
    bb_re = k_re[..., None] * b_re - k_im[..., None] * b_im
    bb_im = k_re[..., None] * b_im + k_im[..., None] * b_re
    eye = jnp.eye(gh, dtype=F32)

    def in_proj(bb):
        bb = bb.reshape(n_half, gh, p, h)
        w = jnp.einsum('ngph,gk->nghkp', bb, eye)
        return w.reshape(n_half, gh * h, gh * p)

    def out_proj(c):
        c = c.reshape(n_half, gh, h, p)
        w = jnp.einsum('nghp,gk->ngpkh', c, eye)
        return w.reshape(n_half, gh * p, gh * h)

    wb = jnp.concatenate([in_proj(bb_re), in_proj(bb_im)], axis=-1).astype(BF16)
    wc = jnp.concatenate([out_proj(c_re), out_proj(-c_im)], axis=1).astype(BF16)
    a = jnp.concatenate([ab_re.reshape(n_half, gh * p), ab_im.reshape(n_half, gh * p)], axis=-1)
    a = jnp.broadcast_to(a[:, None, :], (n_half, nb, 2 * gh * p))
    return a, wb, wc


def _mix_out_kernel(x_ref, at_ref, yf_ref, yb_ref, u_ref, d_ref, wglu_ref, bglu_ref, gs_ref,
                    woa_ref, wos_ref, o_ref):
    y = yf_ref[0] + yb_ref[0] + d_ref[...] * u_ref[0]
    z = jax.nn.gelu(y)
    gate = jnp.dot(z.astype(BF16), wglu_ref[...], preferred_element_type=F32) + bglu_ref[...]
    s = z * jax.nn.sigmoid(gate)
    s_n = _rms(s, gs_ref[...]).astype(BF16)
    mixed = (jnp.dot(at_ref[0], woa_ref[...], preferred_element_type=F32)
             + jnp.dot(s_n, wos_ref[...], preferred_element_type=F32))
    o_ref[0] = x_ref[0] + mixed


def _mix_out(x3, attn_n, y_f, y_b, u, d, w_glu, b_glu, g_s, wo_a, wo_s, *, tm=512):
    b, l, dm = x3.shape
    d_attn = attn_n.shape[-1]
    d_ssm = w_glu.shape[0]
    xrow = pl.BlockSpec((1, tm, dm), lambda bi, i: (bi, i, 0))
    srow = pl.BlockSpec((1, tm, d_ssm), lambda bi, i: (bi, i, 0))
    est = 2 * (w_glu.size + wo_a.size + wo_s.size) + 4 * tm * dm * 4 + 12 * tm * d_ssm * 4
    return pl.pallas_call(
        _mix_out_kernel,
        out_shape=jax.ShapeDtypeStruct((b, l, dm), F32),
        grid=(b, l // tm),
        in_specs=[xrow,
                  pl.BlockSpec((1, tm, d_attn), lambda bi, i: (bi, i, 0)),
                  srow, srow, srow,
                  _const_spec((1, d_ssm)), _const_spec(w_glu.shape), _const_spec((1, d_ssm)),
                  _const_spec((1, d_ssm)), _const_spec(wo_a.shape), _const_spec(wo_s.shape)],
        out_specs=xrow,
        compiler_params=pltpu.CompilerParams(
            dimension_semantics=("arbitrary", "arbitrary"), vmem_limit_bytes=_vmem_limit(est)),
        name="mix_out",
    )(x3, attn_n, y_f, y_b, u, d, w_glu, b_glu, g_s, wo_a, wo_s)


def _ffn_weights(w_gate, w_up, w_down, fc=V7X_MXU_DIM):
    d, dff = w_gate.shape
    n = dff // fc
    wg = w_gate.astype(BF16).reshape(d, n, fc)
    wu = w_up.astype(BF16).reshape(d, n, fc)
    wgu = jnp.concatenate([wg, wu], axis=-1).transpose(1, 0, 2)
    wd = (0.5 * w_down).astype(BF16).reshape(n, fc, d)
    return wgu, wd


def kernel(x, rel_bias_table, ffn1_norm, ffn1_w_gate, ffn1_w_up, ffn1_w_down, mix_norm, w_in, attn_sink, ssm_a_re, ssm_a_im, ssm_log_dt, ssm_b_re, ssm_b_im, ssm_c_re, ssm_c_im, ssm_d, ssm_w_glu, ssm_b_glu, attn_out_norm, ssm_out_norm, w_out, ffn2_norm, ffn2_w_gate, ffn2_w_up, ffn2_w_down, final_norm):
    b, l, dm = x.shape
    depth = w_in.shape[0]
    d_attn = N_HEADS * HEAD_DIM
    d_kv2 = 2 * N_KV_HEADS * HEAD_DIM
    assert b == V7X_SUBLANES, "the scan keeps one batch sequence per sublane"

    bias = _rel_bias(rel_bias_table.astype(F32))
    row = lambda v: v.astype(F32).reshape(1, -1)

    for i in range(depth):
        x2 = _ffn(x.reshape(b * l, dm), row(ffn1_norm[i]),
                  *_ffn_weights(ffn1_w_gate[i], ffn1_w_up[i], ffn1_w_down[i]))
        x = x2.reshape(b, l, dm)

        q, kv, u = _proj(x, row(mix_norm[i]), w_in[i].astype(BF16), d_attn=d_attn, d_kv2=d_kv2)
        attn_n = _attention(q, kv, bias, attn_sink[i].astype(F32), row(attn_out_norm[i]))

        ys = []
        for di in range(2):
            a, wb, wc = _ssm_params(ssm_a_re[i, di], ssm_a_im[i, di], ssm_log_dt[i, di],
                                    ssm_b_re[i, di], ssm_b_im[i, di], ssm_c_re[i, di], ssm_c_im[i, di], b)
            ys.append(_ssm_scan(u, a, wb, wc, reverse=(di == 1)))

        wo = w_out[i].astype(BF16)
        x = _mix_out(x, attn_n, ys[0], ys[1], u, row(ssm_d[i]), ssm_w_glu[i].astype(BF16),
                     row(ssm_b_glu[i]), row(ssm_out_norm[i]), wo[:d_attn], wo[d_attn:])

        fin = row(final_norm) if i == depth - 1 else None
        x2 = _ffn(x.reshape(b * l, dm), row(ffn2_norm[i]),
                  *_ffn_weights(ffn2_w_gate[i], ffn2_w_up[i], ffn2_w_down[i]), fin)
        x = x2.reshape(b, l, dm)
    return x
```

```python
import functools
import math

import jax
import jax.numpy as jnp
from jax import lax
from jax.experimental import pallas as pl
from jax.experimental.pallas import tpu as pltpu

EPS = 1e-6
NEG_INF = -1e30

N_HEADS = 8
N_KV_HEADS = 2
HEAD_DIM = 64
SSM_GROUP = 16
SSM_STATE = 64
ATT_BLOCK = 128
N_BUCKETS = 32
MAX_DISTANCE = 128

V7X_SUBLANES = 8
V7X_LANES = 128
V7X_MXU_DIM = 256
V7X_VMEM_BYTES = 64 * 1024 * 1024

F32 = jnp.float32
BF16 = jnp.bfloat16


def _vmem_limit(nbytes):
    return int(min(nbytes + (8 << 20), V7X_VMEM_BYTES - (6 << 20)))


def _rms(x, g):
    return x * lax.rsqrt(jnp.mean(x * x, axis=-1, keepdims=True) + EPS) * g


def _const_spec(shape):
    nd = len(shape)
    return pl.BlockSpec(shape, lambda *_: (0,) * nd, pipeline_mode=pl.Buffered(1))


def _ffn_kernel(x_ref, g_ref, wg_ref, wu_ref, wd_ref, *rest, fc, final):
    if final:
        fg_ref, o_ref, h_scr, gu_scr = rest
    else:
        o_ref, h_scr, gu_scr = rest
    n_chunks = wd_ref.shape[0] // fc
    x = x_ref[...]
    h_scr[...] = _rms(x, g_ref[...]).astype(BF16)
    o_ref[...] = x

    def gate_up(c):
        cols = slice(c * fc, (c + 1) * fc)
        gu_scr[c % 2, :, :fc] = jnp.dot(h_scr[...], wg_ref[:, cols], preferred_element_type=F32)
        gu_scr[c % 2, :, fc:] = jnp.dot(h_scr[...], wu_ref[:, cols], preferred_element_type=F32)

    def down(c):
        gu = gu_scr[c % 2]
        gate, up = gu[:, :fc], gu[:, fc:]
        act = (gate * jax.nn.sigmoid(gate) * up).astype(BF16)
        o_ref[...] += jnp.dot(act, wd_ref[c * fc:(c + 1) * fc, :], preferred_element_type=F32)

    gate_up(0)
    for c in range(n_chunks):
        if c + 1 < n_chunks:
            gate_up(c + 1)
        down(c)
    if final:
        o_ref[...] = _rms(o_ref[...], fg_ref[...])


def _ffn(x2, g, wg, wu, wd, final_g=None, *, tm=512, fc=V7X_MXU_DIM):
    t, d = x2.shape
    final = final_g is not None
    row = pl.BlockSpec((tm, d), lambda i: (i, 0))
    in_specs = [row, _const_spec((1, d)), _const_spec(wg.shape), _const_spec(wu.shape), _const_spec(wd.shape)]
    args = [x2, g, wg, wu, wd]
    if final:
        in_specs.append(_const_spec((1, d)))
        args.append(final_g)
    est = 2 * (wg.size + wu.size + wd.size) + 4 * tm * d * 4 + tm * d * 2 + 4 * tm * fc * 4
    return pl.pallas_call(
        functools.partial(_ffn_kernel, fc=fc, final=final),
        out_shape=jax.ShapeDtypeStruct((t, d), F32),
        grid=(t // tm,),
        in_specs=in_specs,
        out_specs=row,
        scratch_shapes=[pltpu.VMEM((tm, d), BF16), pltpu.VMEM((2, tm, 2 * fc), F32)],
        compiler_params=pltpu.CompilerParams(
            dimension_semantics=("arbitrary",), vmem_limit_bytes=_vmem_limit(est)),
        name="ffn",
    )(*args)


def _proj_kernel(x_ref, g_ref, w_ref, q_ref, kv_ref, u_ref, *, d_attn, d_kv2):
    h = _rms(x_ref[0], g_ref[...]).astype(BF16)
    p = jnp.dot(h, w_ref[...], preferred_element_type=F32)
    q_ref[0] = (p[:, :d_attn] * (HEAD_DIM ** -0.5)).astype(BF16)
    kv_ref[0] = p[:, d_attn:d_attn + d_kv2].astype(BF16)
    u_ref[0] = p[:, d_attn + d_kv2:]


def _proj(x3, g, w_in, *, d_attn, d_kv2, tm=512):
    b, l, d = x3.shape
    d_in = w_in.shape[1]
    d_ssm = d_in - d_attn - d_kv2
    est = 2 * w_in.size + 2 * tm * d * 4 + 3 * tm * d_in * 4
    return pl.pallas_call(
        functools.partial(_proj_kernel, d_attn=d_attn, d_kv2=d_kv2),
        out_shape=(
            jax.ShapeDtypeStruct((b, l, d_attn), BF16),
            jax.ShapeDtypeStruct((b, l, d_kv2), BF16),
            jax.ShapeDtypeStruct((b, l, d_ssm), F32),
        ),
        grid=(b, l // tm),
        in_specs=[pl.BlockSpec((1, tm, d), lambda bi, i: (bi, i, 0)),
                  _const_spec((1, d)), _const_spec(w_in.shape)],
        out_specs=(
            pl.BlockSpec((1, tm, d_attn), lambda bi, i: (bi, i, 0)),
            pl.BlockSpec((1, tm, d_kv2), lambda bi, i: (bi, i, 0)),
            pl.BlockSpec((1, tm, d_ssm), lambda bi, i: (bi, i, 0)),
        ),
        compiler_params=pltpu.CompilerParams(
            dimension_semantics=("arbitrary", "arbitrary"), vmem_limit_bytes=_vmem_limit(est)),
        name="mix_proj",
    )(x3, g, w_in)


def _bias_kernel(table_ref, idx_ref, band_ref, o_ref):
    idx = idx_ref[...]
    band = band_ref[...] > 0
    for h in range(N_HEADS):
        acc = jnp.zeros(idx.shape, F32)
        for bkt in range(N_BUCKETS):
            acc = jnp.where(idx == bkt, table_ref[bkt, h], acc)
        o_ref[h] = jnp.where(band, acc, NEG_INF)


def _t5_bucket(rel):
    half = N_BUCKETS // 2
    max_exact = half // 2
    ret = jnp.where(rel > 0, half, 0)
    n = jnp.abs(rel)
    nf = jnp.maximum(n, 1).astype(F32)
    large = max_exact + (jnp.log(nf / max_exact) / math.log(MAX_DISTANCE / max_exact)
                         * (half - max_exact)).astype(jnp.int32)
    large = jnp.minimum(large, half - 1)
    return ret + jnp.where(n < max_exact, n, large)


def _rel_bias(table):
    rel = (jnp.arange(3 * ATT_BLOCK)[None, :] - ATT_BLOCK) - jnp.arange(ATT_BLOCK)[:, None]
    idx = _t5_bucket(rel).astype(jnp.int32)
    band = (jnp.abs(rel) <= ATT_BLOCK).astype(jnp.int32)
    return pl.pallas_call(
        _bias_kernel,
        out_shape=jax.ShapeDtypeStruct((N_HEADS, ATT_BLOCK, 3 * ATT_BLOCK), F32),
        in_specs=[pl.BlockSpec(memory_space=pltpu.SMEM),
                  pl.BlockSpec(memory_space=pltpu.VMEM),
                  pl.BlockSpec(memory_space=pltpu.VMEM)],
        out_specs=pl.BlockSpec(memory_space=pltpu.VMEM),
        name="rel_bias",
    )(table, idx, band)


def _attn_kernel(sink_ref, q_ref, kvp_ref, kvm_ref, kvn_ref, bias_ref, g_ref, o_ref,
                 kd_scr, ve_scr, vo_scr, *, nq):
    i = pl.program_id(1)
    n_i = pl.num_programs(1)
    blk = ATT_BLOCK
    dh = HEAD_DIM
    dk2 = N_KV_HEADS * dh
    grp = N_HEADS // N_KV_HEADS
    assert dk2 == V7X_LANES and grp == 4
    lane = lax.broadcasted_iota(jnp.int32, (1, dk2), 1)
    m_lo = (lane < dh).astype(BF16)
    m_hi = (lane >= dh).astype(BF16)

    kv = jnp.concatenate([kvp_ref[0], kvm_ref[0], kvn_ref[0]], axis=0)
    rows = kv.shape[0]
    k_row, v_row = kv[:, :dk2], kv[:, dk2:]
    k_swap = jnp.concatenate([k_row[:, dh:], k_row[:, :dh]], axis=-1)
    v_swap = jnp.concatenate([v_row[:, dh:], v_row[:, :dh]], axis=-1)
    ones_lo = jnp.broadcast_to(m_lo, (rows, dk2))
    ones_hi = jnp.broadcast_to(m_hi, (rows, dk2))
    kd_scr[0] = k_row * m_lo + k_swap * m_hi
    kd_scr[1] = k_swap * m_lo + k_row * m_hi
    ve_scr[0] = jnp.concatenate([v_row * m_lo, ones_lo], axis=-1)
    vo_scr[0] = jnp.concatenate([v_swap * m_hi, ones_hi], axis=-1)
    ve_scr[1] = jnp.concatenate([v_swap * m_lo, ones_lo], axis=-1)
    vo_scr[1] = jnp.concatenate([v_row * m_hi, ones_hi], axis=-1)

    col = lax.broadcasted_iota(jnp.int32, (1, 3 * blk), 1)
    edge_first = jnp.where(col < blk, jnp.where(i == 0, NEG_INF, 0.0), 0.0)
    edge_last = jnp.where(col >= 2 * blk, jnp.where(i == n_i - 1, NEG_INF, 0.0), 0.0)
    lane_lo = lane < dh

    for jb in range(nq):
        win = slice(jb * blk, (jb + 3) * blk)
        pieces = []
        for kh in range(N_KV_HEADS):
            qs = []
            for msk in (m_lo, m_hi):
                for pr in range(grp // 2):
                    c0 = kh * grp * dh + pr * dk2
                    qs.append(q_ref[0, jb * blk:(jb + 1) * blk, c0:c0 + dk2] * msk)
            q_st = jnp.concatenate(qs, axis=0)
            s = lax.dot_general(q_st, kd_scr[kh, win, :], (((1,), (1,)), ((), ())),
                                preferred_element_type=F32)
            s = s + bias_ref[kh]
            if jb == 0:
                s = s + edge_first
            if jb == nq - 1:
                s = s + edge_last
            s_max = jnp.maximum(jnp.maximum(s[:, :blk], s[:, blk:2 * blk]), s[:, 2 * blk:])
            p, t = [], []
            for r, hh in enumerate((0, 2, 1, 3)):
                rs = slice(r * blk, (r + 1) * blk)
                sink = sink_ref[kh * grp + hh]
                m = jnp.maximum(jnp.max(s_max[rs], axis=-1, keepdims=True), sink)
                p.append(jnp.exp(s[rs] - m).astype(BF16))
                t.append(jnp.exp(sink - m))
            for pr in range(grp // 2):
                ev, od = pr, grp // 2 + pr
                nd = (jnp.dot(p[ev], ve_scr[kh, win, :], preferred_element_type=F32)
                      + jnp.dot(p[od], vo_scr[kh, win, :], preferred_element_type=F32))
                den = nd[:, dk2:] + jnp.where(lane_lo, t[ev], t[od])
                pieces.append(nd[:, :dk2] / den)
        o = jnp.concatenate(pieces, axis=-1)
        o_ref[0, jb * blk:(jb + 1) * blk, :] = _rms(o, g_ref[...]).astype(BF16)


def _attention(q, kv, bias, sink, g, *, nq=4):
    b, l, d_attn = q.shape
    d_kv2 = kv.shape[-1]
    blk = ATT_BLOCK
    tq = nq * blk
    nb = l // blk
    grp = N_HEADS // N_KV_HEADS
    bias_g = bias.reshape(N_KV_HEADS, grp, blk, 3 * blk)[:, jnp.array([0, 2, 1, 3])]
    bias_g = bias_g.reshape(N_KV_HEADS, grp * blk, 3 * blk)
    rows = tq + 2 * blk
    est = (bias.size * 4 + 6 * tq * (d_attn + d_kv2) * 2 + N_KV_HEADS * rows * 5 * V7X_LANES * 2
           + 6 * grp * blk * 3 * blk * 4)
    return pl.pallas_call(
        functools.partial(_attn_kernel, nq=nq),
        out_shape=jax.ShapeDtypeStruct((b, l, d_attn), BF16),
        grid=(b, l // tq),
        in_specs=[
            pl.BlockSpec(memory_space=pltpu.SMEM),
            pl.BlockSpec((1, tq, d_attn), lambda bi, i: (bi, i, 0)),
            pl.BlockSpec((1, blk, d_kv2), lambda bi, i: (bi, jnp.maximum(i * nq - 1, 0), 0)),
            pl.BlockSpec((1, tq, d_kv2), lambda bi, i: (bi, i, 0)),
            pl.BlockSpec((1, blk, d_kv2), lambda bi, i: (bi, jnp.minimum((i + 1) * nq, nb - 1), 0)),
            _const_spec(bias_g.shape),
            _const_spec((1, d_attn)),
        ],
        out_specs=pl.BlockSpec((1, tq, d_attn), lambda bi, i: (bi, i, 0)),
        scratch_shapes=[pltpu.VMEM((N_KV_HEADS, rows, V7X_LANES), BF16),
                        pltpu.VMEM((N_KV_HEADS, rows, 2 * V7X_LANES), BF16),
                        pltpu.VMEM((N_KV_HEADS, rows, 2 * V7X_LANES), BF16)],
        compiler_params=pltpu.CompilerParams(
            dimension_semantics=("arbitrary", "arbitrary"), vmem_limit_bytes=_vmem_limit(est)),
        name="band_attn",
    )(sink, q, kv, kv, kv, bias_g, g)


def _ssm_kernel(u_ref, a_ref, wb_ref, wc_ref, y_ref, ut_scr, yt_scr, x_scr, s_scr, st_scr,
                *, tc, nb, reverse, lane_tile):
    n_half, _, n2 = a_ref.shape
    nre = n2 // 2
    kh = wb_ref.shape[1]
    rows2 = 2 * nb
    n_pairs = tc // 2
    n_slab = ut_scr.shape[0]
    slab_per_half = kh // V7X_LANES

    @pl.when(pl.program_id(0) == 0)
    def _():
        st_scr[...] = jnp.zeros_like(st_scr)

    for bi in range(nb):
        for sl in range(n_slab):
            ut_scr.at[sl][pl.ds(bi, tc, stride=nb), :] = u_ref[bi, :, sl * V7X_LANES:(sl + 1) * V7X_LANES]

    for hf in range(n_half):
        u_h = jnp.concatenate([ut_scr[hf * slab_per_half + k] for k in range(slab_per_half)], axis=-1)
        x_scr[hf] = jnp.dot(u_h.astype(BF16), wb_ref[hf], preferred_element_type=F32)
    for hf in range(n_half):
        for lt in range(nre // lane_tile):
            re_sl = slice(lt * lane_tile, (lt + 1) * lane_tile)
            im_sl = slice(nre + lt * lane_tile, nre + (lt + 1) * lane_tile)
            a_re = a_ref[hf, :, re_sl]
            a_im = a_ref[hf, :, im_sl]

            def step(s_re, s_im, x_re, x_im):
                n_re = a_re * s_re - a_im * s_im + x_re
                n_im = a_re * s_im + a_im * s_re + x_im
                return n_re, n_im

            s_re, s_im = st_scr[hf, :, re_sl], st_scr[hf, :, im_sl]
            for j in range(n_pairs):
                r0 = ((n_pairs - 1 - j) if reverse else j) * rows2
                x_re = x_scr[hf, r0:r0 + rows2, re_sl]
                x_im = x_scr[hf, r0:r0 + rows2, im_sl]
                first, second = (slice(nb, rows2), slice(0, nb)) if reverse else (slice(0, nb), slice(nb, rows2))
                s1_re, s1_im = step(s_re, s_im, x_re[first], x_im[first])
                s_re, s_im = step(s1_re, s1_im, x_re[second], x_im[second])
                lo_re, hi_re = (s_re, s1_re) if reverse else (s1_re, s_re)
                lo_im, hi_im = (s_im, s1_im) if reverse else (s1_im, s_im)
                s_scr[hf, r0:r0 + rows2, re_sl] = jnp.concatenate([lo_re, hi_re], axis=0).astype(BF16)
                s_scr[hf, r0:r0 + rows2, im_sl] = jnp.concatenate([lo_im, hi_im], axis=0).astype(BF16)
            st_scr[hf, :, re_sl] = s_re
            st_scr[hf, :, im_sl] = s_im
        mh = (tc * nb) // 2
        for r in range(2):
            y_h = jnp.dot(s_scr[hf, r * mh:(r + 1) * mh, :], wc_ref[hf], preferred_element_type=F32)
            for k in range(slab_per_half):
                yt_scr[hf * slab_per_half + k, r * mh:(r + 1) * mh, :] = y_h[:, k * V7X_LANES:(k + 1) * V7X_LANES]

    for bi in range(nb):
        for sl in range(n_slab):
            y_ref[bi, :, sl * V7X_LANES:(sl + 1) * V7X_LANES] = yt_scr.at[sl][pl.ds(bi, tc, stride=nb), :]


def _ssm_scan(u, a, wb, wc, *, reverse, tc=64, lane_tile=512):
    nb, l, d_ssm = u.shape
    n_half, kh, n2 = wb.shape
    m = tc * nb
    n_t = l // tc
    idx = (lambda i: (0, n_t - 1 - i, 0)) if reverse else (lambda i: (0, i, 0))
    est = 6 * m * d_ssm * 4 + m * n2 * 6 + 2 * (wb.size + wc.size) + 3 * a.size * 4
    return pl.pallas_call(
        functools.partial(_ssm_kernel, tc=tc, nb=nb, reverse=reverse, lane_tile=lane_tile),
        out_shape=jax.ShapeDtypeStruct((nb, l, d_ssm), F32),
        grid=(n_t,),
        in_specs=[pl.BlockSpec((nb, tc, d_ssm), idx),
                  _const_spec(a.shape), _const_spec(wb.shape), _const_spec(wc.shape)],
        out_specs=pl.BlockSpec((nb, tc, d_ssm), idx),
        scratch_shapes=[pltpu.VMEM((d_ssm // V7X_LANES, m, V7X_LANES), F32),
                        pltpu.VMEM((d_ssm // V7X_LANES, m, V7X_LANES), F32),
                        pltpu.VMEM((n_half, m, n2), F32),
                        pltpu.VMEM((n_half, m, n2), BF16),
                        pltpu.VMEM((n_half, nb, n2), F32)],
        compiler_params=pltpu.CompilerParams(
            dimension_semantics=("arbitrary",), vmem_limit_bytes=_vmem_limit(est)),
        name="s5_scan_bwd" if reverse else "s5_scan_fwd",
    )(u, a, wb, wc)


def _ssm_params(a_re, a_im, log_dt, b_re, b_im, c_re, c_im, nb):
    g, p = a_re.shape
    h = b_re.shape[-1]
    n_half = (g * h) // V7X_MXU_DIM
    gh = g // n_half
    dt = jnp.exp(log_dt)[:, None]
    mag = jnp.exp(dt * a_re)
    ab_re = mag * jnp.cos(dt * a_im)
    ab_im = mag * jnp.sin(dt * a_im)
    den = a_re * a_re + a_im * a_im
    nr, ni = ab_re - 1.0, ab_im
    k_re = (nr * a_re + ni * a_im) / den
    k_im = (ni * a_re - nr * a_im) / den
    bb_re = k_re[..., None] * b_re - k_im[..., None] * b_im
    bb_im = k_re[..., None] * b_im + k_im[..., None] * b_re
    eye = jnp.eye(gh, dtype=F32)

    def in_proj(bb):
        bb = bb.reshape(n_half, gh, p, h)
        w = jnp.einsum('ngph,gk->nghkp', bb, eye)
        return w.reshape(n_half, gh * h, gh * p)

    def out_proj(c):
        c = c.reshape(n_half, gh, h, p)
        w = jnp.einsum('nghp,gk->ngpkh', c, eye)
        return w.reshape(n_half, gh * p, gh * h)

    wb = jnp.concatenate([in_proj(bb_re), in_proj(bb_im)], axis=-1).astype(BF16)
    wc = jnp.concatenate([out_proj(c_re), out_proj(-c_im)], axis=1).astype(BF16)
    a = jnp.concatenate([ab_re.reshape(n_half, gh * p), ab_im.reshape(n_half, gh * p)], axis=-1)
    a = jnp.broadcast_to(a[:, None, :], (n_half, nb, 2 * gh * p))
    return a, wb, wc


def _mix_out_kernel(x_ref, at_ref, yf_ref, yb_ref, u_ref, d_ref, wglu_ref, bglu_ref, gs_ref,
                    woa_ref, wos_ref, o_ref):
    y = yf_ref[0] + yb_ref[0] + d_ref[...] * u_ref[0]
    z = jax.nn.gelu(y)
    gate = jnp.dot(z.astype(BF16), wglu_ref[...], preferred_element_type=F32) + bglu_ref[...]
    s = z * jax.nn.sigmoid(gate)
    s_n = _rms(s, gs_ref[...]).astype(BF16)
    mixed = (jnp.dot(at_ref[0], woa_ref[...], preferred_element_type=F32)
             + jnp.dot(s_n, wos_ref[...], preferred_element_type=F32))
    o_ref[0] = x_ref[0] + mixed


def _mix_out(x3, attn_n, y_f, y_b, u, d, w_glu, b_glu, g_s, wo_a, wo_s, *, tm=512):
    b, l, dm = x3.shape
    d_attn = attn_n.shape[-1]
    d_ssm = w_glu.shape[0]
    xrow = pl.BlockSpec((1, tm, dm), lambda bi, i: (bi, i, 0))
    srow = pl.BlockSpec((1, tm, d_ssm), lambda bi, i: (bi, i, 0))
    est = 2 * (w_glu.size + wo_a.size + wo_s.size) + 4 * tm * dm * 4 + 12 * tm * d_ssm * 4
    return pl.pallas_call(
        _mix_out_kernel,
        out_shape=jax.ShapeDtypeStruct((b, l, dm), F32),
        grid=(b, l // tm),
        in_specs=[xrow,
                  pl.BlockSpec((1, tm, d_attn), lambda bi, i: (bi, i, 0)),
                  srow, srow, srow,
                  _const_spec((1, d_ssm)), _const_spec(w_glu.shape), _const_spec((1, d_ssm)),
                  _const_spec((1, d_ssm)), _const_spec(wo_a.shape), _const_spec(wo_s.shape)],
        out_specs=xrow,
        compiler_params=pltpu.CompilerParams(
            dimension_semantics=("arbitrary", "arbitrary"), vmem_limit_bytes=_vmem_limit(est)),
        name="mix_out",
    )(x3, attn_n, y_f, y_b, u, d, w_glu, b_glu, g_s, wo_a, wo_s)


def _ffn_weights(w_gate, w_up, w_down):
    return w_gate.astype(BF16), w_up.astype(BF16), (0.5 * w_down).astype(BF16)


def kernel(x, rel_bias_table, ffn1_norm, ffn1_w_gate, ffn1_w_up, ffn1_w_down, mix_norm, w_in, attn_sink, ssm_a_re, ssm_a_im, ssm_log_dt, ssm_b_re, ssm_b_im, ssm_c_re, ssm_c_im, ssm_d, ssm_w_glu, ssm_b_glu, attn_out_norm, ssm_out_norm, w_out, ffn2_norm, ffn2_w_gate, ffn2_w_up, ffn2_w_down, final_norm):
    b, l, dm = x.shape
    depth = w_in.shape[0]
    d_attn = N_HEADS * HEAD_DIM
    d_kv2 = 2 * N_KV_HEADS * HEAD_DIM
    assert b == V7X_SUBLANES, "the scan keeps one batch sequence per sublane"

    bias = _rel_bias(rel_bias_table.astype(F32))
    row = lambda v: v.astype(F32).reshape(1, -1)

    for i in range(depth):
        x2 = _ffn(x.reshape(b * l, dm), row(ffn1_norm[i]),
                  *_ffn_weights(ffn1_w_gate[i], ffn1_w_up[i], ffn1_w_down[i]))
        x = x2.reshape(b, l, dm)

        q, kv, u = _proj(x, row(mix_norm[i]), w_in[i].astype(BF16), d_attn=d_attn, d_kv2=d_kv2)
        attn_n = _attention(q, kv, bias, attn_sink[i].astype(F32), row(attn_out_norm[i]))

        ys = []
        for di in range(2):
            a, wb, wc = _ssm_params(ssm_a_re[i, di], ssm_a_im[i, di], ssm_log_dt[i, di],
                                    ssm_b_re[i, di], ssm_b_im[i, di], ssm_c_re[i, di], ssm_c_im[i, di], b)
            ys.append(_ssm_scan(u, a, wb, wc, reverse=(di == 1)))

        wo = w_out[i].astype(BF16)
        x = _mix_out(x, attn_n, ys[0], ys[1], u, row(ssm_d[i]), ssm_w_glu[i].astype(BF16),
                     row(ssm_b_glu[i]), row(ssm_out_norm[i]), wo[:d_attn], wo[d_attn:])

        fin = row(final_norm) if i == depth - 1 else None
        x2 = _ffn(x.reshape(b * l, dm), row(ffn2_norm[i]),
                  *_ffn_weights(ffn2_w_gate[i], ffn2_w_up[i], ffn2_w_down[i]), fin)
        x = x2.reshape(b, l, dm)
    return x
```

```python
import functools
import math

import jax
import jax.numpy as jnp
from jax import lax
from jax.experimental import pallas as pl
from jax.experimental.pallas import tpu as pltpu

EPS = 1e-6
NEG_INF = -1e30

N_HEADS = 8
N_KV_HEADS = 2
HEAD_DIM = 64
SSM_GROUP = 16
SSM_STATE = 64
ATT_BLOCK = 128
N_BUCKETS = 32
MAX_DISTANCE = 128

V7X_SUBLANES = 8
V7X_LANES = 128
V7X_MXU_DIM = 256
V7X_VMEM_BYTES = 64 * 1024 * 1024

F32 = jnp.float32
BF16 = jnp.bfloat16


def _vmem_limit(nbytes):
    return int(min(nbytes + (8 << 20), V7X_VMEM_BYTES - (6 << 20)))


def _rms(x, g):
    return x * lax.rsqrt(jnp.mean(x * x, axis=-1, keepdims=True) + EPS) * g


def _const_spec(shape):
    nd = len(shape)
    return pl.BlockSpec(shape, lambda *_: (0,) * nd, pipeline_mode=pl.Buffered(1))


def _layer_spec(arr, *idx):
    rest = arr.shape[len(idx):]
    return pl.BlockSpec((None,) * len(idx) + rest, lambda *_: idx + (0,) * len(rest),
                        pipeline_mode=pl.Buffered(1))


def _ffn_kernel(x_ref, g_ref, wg_ref, wu_ref, wd_ref, *rest, fc, final):
    if final:
        fg_ref, o_ref, h_scr, gu_scr = rest
    else:
        o_ref, h_scr, gu_scr = rest
    n_chunks = wd_ref.shape[0] // fc
    x = x_ref[...]
    h_scr[...] = _rms(x, g_ref[...]).astype(BF16)
    o_ref[...] = x

    def gate_up(c):
        cols = slice(c * fc, (c + 1) * fc)
        gu_scr[c % 2, :, :fc] = jnp.dot(h_scr[...], wg_ref[:, cols], preferred_element_type=F32)
        gu_scr[c % 2, :, fc:] = jnp.dot(h_scr[...], wu_ref[:, cols], preferred_element_type=F32)

    def down(c):
        gu = gu_scr[c % 2]
        gate, up = gu[:, :fc], gu[:, fc:]
        act = (gate * jax.nn.sigmoid(gate) * up).astype(BF16)
        o_ref[...] += jnp.dot(act, wd_ref[c * fc:(c + 1) * fc, :], preferred_element_type=F32)

    gate_up(0)
    for c in range(n_chunks):
        if c + 1 < n_chunks:
            gate_up(c + 1)
        down(c)
    if final:
        o_ref[...] = _rms(o_ref[...], fg_ref[...])


def _ffn(x2, params, layer, final_g=None, *, tm=512, fc=V7X_MXU_DIM):
    t, d = x2.shape
    final = final_g is not None
    row = pl.BlockSpec((tm, d), lambda i: (i, 0))
    in_specs = [row] + [_layer_spec(p, layer) for p in params]
    args = [x2, *params]
    if final:
        in_specs.append(_const_spec((1, d)))
        args.append(final_g)
    est = 2 * sum(p[0].size for p in params[1:]) + 4 * tm * d * 4 + tm * d * 2 + 4 * tm * fc * 4
    return pl.pallas_call(
        functools.partial(_ffn_kernel, fc=fc, final=final),
        out_shape=jax.ShapeDtypeStruct((t, d), F32),
        grid=(t // tm,),
        in_specs=in_specs,
        out_specs=row,
        scratch_shapes=[pltpu.VMEM((tm, d), BF16), pltpu.VMEM((2, tm, 2 * fc), F32)],
        compiler_params=pltpu.CompilerParams(
            dimension_semantics=("arbitrary",), vmem_limit_bytes=_vmem_limit(est)),
        name="ffn",
    )(*args)


def _proj_kernel(x_ref, g_ref, w_ref, q_ref, kv_ref, u_ref, *, d_attn, d_kv2):
    h = _rms(x_ref[0], g_ref[...]).astype(BF16)
    p = jnp.dot(h, w_ref[...], preferred_element_type=F32)
    q_ref[0] = (p[:, :d_attn] * (HEAD_DIM ** -0.5)).astype(BF16)
    kv_ref[0] = p[:, d_attn:d_attn + d_kv2].astype(BF16)
    u_ref[0] = p[:, d_attn + d_kv2:]


def _proj(x3, g, w_in, layer, *, d_attn, d_kv2, tm=512):
    b, l, d = x3.shape
    d_in = w_in.shape[-1]
    d_ssm = d_in - d_attn - d_kv2
    est = 2 * w_in[0].size + 2 * tm * d * 4 + 3 * tm * d_in * 4
    return pl.pallas_call(
        functools.partial(_proj_kernel, d_attn=d_attn, d_kv2=d_kv2),
        out_shape=(
            jax.ShapeDtypeStruct((b, l, d_attn), BF16),
            jax.ShapeDtypeStruct((b, l, d_kv2), BF16),
            jax.ShapeDtypeStruct((b, l, d_ssm), F32),
        ),
        grid=(b, l // tm),
        in_specs=[pl.BlockSpec((1, tm, d), lambda bi, i: (bi, i, 0)),
                  _layer_spec(g, layer), _layer_spec(w_in, layer)],
        out_specs=(
            pl.BlockSpec((1, tm, d_attn), lambda bi, i: (bi, i, 0)),
            pl.BlockSpec((1, tm, d_kv2), lambda bi, i: (bi, i, 0)),
            pl.BlockSpec((1, tm, d_ssm), lambda bi, i: (bi, i, 0)),
        ),
        compiler_params=pltpu.CompilerParams(
            dimension_semantics=("arbitrary", "arbitrary"), vmem_limit_bytes=_vmem_limit(est)),
        name="mix_proj",
    )(x3, g, w_in)


def _bias_kernel(table_ref, idx_ref, band_ref, o_ref):
    idx = idx_ref[...]
    band = band_ref[...] > 0
    for h in range(N_HEADS):
        acc = jnp.zeros(idx.shape, F32)
        for bkt in range(N_BUCKETS):
            acc = jnp.where(idx == bkt, table_ref[bkt, h], acc)
        o_ref[h] = jnp.where(band, acc, NEG_INF)


def _t5_bucket(rel):
    half = N_BUCKETS // 2
    max_exact = half // 2
    ret = jnp.where(rel > 0, half, 0)
    n = jnp.abs(rel)
    nf = jnp.maximum(n, 1).astype(F32)
    large = max_exact + (jnp.log(nf / max_exact) / math.log(MAX_DISTANCE / max_exact)
                         * (half - max_exact)).astype(jnp.int32)
    large = jnp.minimum(large, half - 1)
    return ret + jnp.where(n < max_exact, n, large)


def _rel_bias(table):
    rel = (jnp.arange(3 * ATT_BLOCK)[None, :] - ATT_BLOCK) - jnp.arange(ATT_BLOCK)[:, None]
    idx = _t5_bucket(rel).astype(jnp.int32)
    band = (jnp.abs(rel) <= ATT_BLOCK).astype(jnp.int32)
    return pl.pallas_call(
        _bias_kernel,
        out_shape=jax.ShapeDtypeStruct((N_HEADS, ATT_BLOCK, 3 * ATT_BLOCK), F32),
        in_specs=[pl.BlockSpec(memory_space=pltpu.SMEM),
                  pl.BlockSpec(memory_space=pltpu.VMEM),
                  pl.BlockSpec(memory_space=pltpu.VMEM)],
        out_specs=pl.BlockSpec(memory_space=pltpu.VMEM),
        name="rel_bias",
    )(table, idx, band)


def _attn_kernel(sink_ref, q_ref, kvp_ref, kvm_ref, kvn_ref, bias_ref, g_ref, o_ref,
                 kd_scr, ve_scr, vo_scr, s_scr, *, nq, layer):
    i = pl.program_id(1)
    n_i = pl.num_programs(1)
    blk = ATT_BLOCK
    dh = HEAD_DIM
    dk2 = N_KV_HEADS * dh
    grp = N_HEADS // N_KV_HEADS
    assert dk2 == V7X_LANES and grp == 4
    lane = lax.broadcasted_iota(jnp.int32, (1, dk2), 1)
    m_lo = (lane < dh).astype(BF16)
    m_hi = (lane >= dh).astype(BF16)

    kv = jnp.concatenate([kvp_ref[0], kvm_ref[0], kvn_ref[0]], axis=0)
    rows = kv.shape[0]
    k_row, v_row = kv[:, :dk2], kv[:, dk2:]
    k_swap = jnp.concatenate([k_row[:, dh:], k_row[:, :dh]], axis=-1)
    v_swap = jnp.concatenate([v_row[:, dh:], v_row[:, :dh]], axis=-1)
    ones_lo = jnp.broadcast_to(m_lo, (rows, dk2))
    ones_hi = jnp.broadcast_to(m_hi, (rows, dk2))
    kd_scr[0] = k_row * m_lo + k_swap * m_hi
    kd_scr[1] = k_swap * m_lo + k_row * m_hi
    ve_scr[0] = jnp.concatenate([v_row * m_lo, ones_lo], axis=-1)
    vo_scr[0] = jnp.concatenate([v_swap * m_hi, ones_hi], axis=-1)
    ve_scr[1] = jnp.concatenate([v_swap * m_lo, ones_lo], axis=-1)
    vo_scr[1] = jnp.concatenate([v_row * m_hi, ones_hi], axis=-1)

    col = lax.broadcasted_iota(jnp.int32, (1, 3 * blk), 1)
    edge_first = jnp.where(col < blk, jnp.where(i == 0, NEG_INF, 0.0), 0.0)
    edge_last = jnp.where(col >= 2 * blk, jnp.where(i == n_i - 1, NEG_INF, 0.0), 0.0)
    lane_lo = lane < dh

    def scores(jb, kh):
        qs = []
        for msk in (m_lo, m_hi):
            for pr in range(grp // 2):
                c0 = kh * grp * dh + pr * dk2
                qs.append(q_ref[0, jb * blk:(jb + 1) * blk, c0:c0 + dk2] * msk)
        q_st = jnp.concatenate(qs, axis=0)
        s = lax.dot_general(q_st, kd_scr[kh, jb * blk:(jb + 3) * blk, :], (((1,), (1,)), ((), ())),
                            preferred_element_type=F32)
        s = s + bias_ref[kh]
        if jb == 0:
            s = s + edge_first
        if jb == nq - 1:
            s = s + edge_last
        return s

    tiles = [(jb, kh) for jb in range(nq) for kh in range(N_KV_HEADS)]
    s_scr[0] = scores(*tiles[0])
    pieces = []
    for n, (jb, kh) in enumerate(tiles):
        win = slice(jb * blk, (jb + 3) * blk)
        if n + 1 < len(tiles):
            s_scr[(n + 1) % 2] = scores(*tiles[n + 1])
        s = s_scr[n % 2]
        s_max = jnp.maximum(jnp.maximum(s[:, :blk], s[:, blk:2 * blk]), s[:, 2 * blk:])
        p, t = [], []
        for r, hh in enumerate((0, 2, 1, 3)):
            rs = slice(r * blk, (r + 1) * blk)
            sink = sink_ref[layer, kh * grp + hh]
            m = jnp.maximum(jnp.max(s_max[rs], axis=-1, keepdims=True), sink)
            p.append(jnp.exp(s[rs] - m).astype(BF16))
            t.append(jnp.exp(sink - m))
        for pr in range(grp // 2):
            ev, od = pr, grp // 2 + pr
            nd = (jnp.dot(p[ev], ve_scr[kh, win, :], preferred_element_type=F32)
                  + jnp.dot(p[od], vo_scr[kh, win, :], preferred_element_type=F32))
            den = nd[:, dk2:] + jnp.where(lane_lo, t[ev], t[od])
            pieces.append(nd[:, :dk2] / den)
        if kh == N_KV_HEADS - 1:
            o = jnp.concatenate(pieces, axis=-1)
            o_ref[0, jb * blk:(jb + 1) * blk, :] = _rms(o, g_ref[...]).astype(BF16)
            pieces = []


def _group_bias(bias):
    grp = N_HEADS // N_KV_HEADS
    bias_g = bias.reshape(N_KV_HEADS, grp, ATT_BLOCK, 3 * ATT_BLOCK)[:, jnp.array([0, 2, 1, 3])]
    return bias_g.reshape(N_KV_HEADS, grp * ATT_BLOCK, 3 * ATT_BLOCK)


def _attention(q, kv, bias_g, sink, g, layer, *, nq=4):
    b, l, d_attn = q.shape
    d_kv2 = kv.shape[-1]
    blk = ATT_BLOCK
    tq = nq * blk
    nb = l // blk
    grp = N_HEADS // N_KV_HEADS
    rows = tq + 2 * blk
    est = (bias_g.size * 4 + 6 * tq * (d_attn + d_kv2) * 2 + N_KV_HEADS * rows * 5 * V7X_LANES * 2
           + 6 * grp * blk * 3 * blk * 4)
    return pl.pallas_call(
        functools.partial(_attn_kernel, nq=nq, layer=layer),
        out_shape=jax.ShapeDtypeStruct((b, l, d_attn), BF16),
        grid=(b, l // tq),
        in_specs=[
            pl.BlockSpec(memory_space=pltpu.SMEM),
            pl.BlockSpec((1, tq, d_attn), lambda bi, i: (bi, i, 0)),
            pl.BlockSpec((1, blk, d_kv2), lambda bi, i: (bi, jnp.maximum(i * nq - 1, 0), 0)),
            pl.BlockSpec((1, tq, d_kv2), lambda bi, i: (bi, i, 0)),
            pl.BlockSpec((1, blk, d_kv2), lambda bi, i: (bi, jnp.minimum((i + 1) * nq, nb - 1), 0)),
            _const_spec(bias_g.shape),
            _layer_spec(g, layer),
        ],
        out_specs=pl.BlockSpec((1, tq, d_attn), lambda bi, i: (bi, i, 0)),
        scratch_shapes=[pltpu.VMEM((N_KV_HEADS, rows, V7X_LANES), BF16),
                        pltpu.VMEM((N_KV_HEADS, rows, 2 * V7X_LANES), BF16),
                        pltpu.VMEM((N_KV_HEADS, rows, 2 * V7X_LANES), BF16),
                        pltpu.VMEM((2, grp * blk, 3 * blk), F32)],
        compiler_params=pltpu.CompilerParams(
            dimension_semantics=("arbitrary", "arbitrary"), vmem_limit_bytes=_vmem_limit(est)),
        name="band_attn",
    )(sink, q, kv, kv, kv, bias_g, g)


def _ssm_kernel(u_ref, a_ref, wb_ref, wc_ref, y_ref, ut_scr, yt_scr, x_scr, s_scr, st_scr,
                *, tc, nb, reverse, lane_tile):
    n_half, _, n2 = a_ref.shape
    nre = n2 // 2
    kh = wb_ref.shape[1]
    rows2 = 2 * nb
    n_pairs = tc // 2
    n_slab = ut_scr.shape[0]
    slab_per_half = kh // V7X_LANES

    @pl.when(pl.program_id(0) == 0)
    def _():
        st_scr[...] = jnp.zeros_like(st_scr)

    for bi in range(nb):
        for sl in range(n_slab):
            ut_scr.at[sl][pl.ds(bi, tc, stride=nb), :] = u_ref[bi, :, sl * V7X_LANES:(sl + 1) * V7X_LANES]

    for hf in range(n_half):
        u_h = jnp.concatenate([ut_scr[hf * slab_per_half + k] for k in range(slab_per_half)], axis=-1)
        x_scr[hf] = jnp.dot(u_h.astype(BF16), wb_ref[hf], preferred_element_type=F32)
    for hf in range(n_half):
        for lt in range(nre // lane_tile):
            re_sl = slice(lt * lane_tile, (lt + 1) * lane_tile)
            im_sl = slice(nre + lt * lane_tile, nre + (lt + 1) * lane_tile)
            a_re = a_ref[hf, :, re_sl]
            a_im = a_ref[hf, :, im_sl]

            def step(s_re, s_im, x_re, x_im):
                n_re = a_re * s_re - a_im * s_im + x_re
                n_im = a_re * s_im + a_im * s_re + x_im
                return n_re, n_im

            s_re, s_im = st_scr[hf, :, re_sl], st_scr[hf, :, im_sl]
            for j in range(n_pairs):
                r0 = ((n_pairs - 1 - j) if reverse else j) * rows2
                x_re = x_scr[hf, r0:r0 + rows2, re_sl]
                x_im = x_scr[hf, r0:r0 + rows2, im_sl]
                first, second = (slice(nb, rows2), slice(0, nb)) if reverse else (slice(0, nb), slice(nb, rows2))
                s1_re, s1_im = step(s_re, s_im, x_re[first], x_im[first])
                s_re, s_im = step(s1_re, s1_im, x_re[second], x_im[second])
                lo_re, hi_re = (s_re, s1_re) if reverse else (s1_re, s_re)
                lo_im, hi_im = (s_im, s1_im) if reverse else (s1_im, s_im)
                s_scr[hf, r0:r0 + rows2, re_sl] = jnp.concatenate([lo_re, hi_re], axis=0).astype(BF16)
                s_scr[hf, r0:r0 + rows2, im_sl] = jnp.concatenate([lo_im, hi_im], axis=0).astype(BF16)
            st_scr[hf, :, re_sl] = s_re
            st_scr[hf, :, im_sl] = s_im
        mh = (tc * nb) // 2
        for r in range(2):
            y_h = jnp.dot(s_scr[hf, r * mh:(r + 1) * mh, :], wc_ref[hf], preferred_element_type=F32)
            for k in range(slab_per_half):
                yt_scr[hf * slab_per_half + k, r * mh:(r + 1) * mh, :] = y_h[:, k * V7X_LANES:(k + 1) * V7X_LANES]

    for bi in range(nb):
        for sl in range(n_slab):
            y_ref[bi, :, sl * V7X_LANES:(sl + 1) * V7X_LANES] = yt_scr.at[sl][pl.ds(bi, tc, stride=nb), :]


def _ssm_scan(u, a, wb, wc, layer, *, reverse, tc=64, lane_tile=512):
    nb, l, d_ssm = u.shape
    n_half, kh, n2 = wb.shape[2:]
    m = tc * nb
    n_t = l // tc
    di = int(reverse)
    idx = (lambda i: (0, n_t - 1 - i, 0)) if reverse else (lambda i: (0, i, 0))
    est = (6 * m * d_ssm * 4 + n_half * m * n2 * 6
           + 2 * (wb[0, 0].size + wc[0, 0].size) + 4 * a[0, 0].size)
    return pl.pallas_call(
        functools.partial(_ssm_kernel, tc=tc, nb=nb, reverse=reverse, lane_tile=lane_tile),
        out_shape=jax.ShapeDtypeStruct((nb, l, d_ssm), F32),
        grid=(n_t,),
        in_specs=[pl.BlockSpec((nb, tc, d_ssm), idx),
                  _layer_spec(a, layer, di), _layer_spec(wb, layer, di), _layer_spec(wc, layer, di)],
        out_specs=pl.BlockSpec((nb, tc, d_ssm), idx),
        scratch_shapes=[pltpu.VMEM((d_ssm // V7X_LANES, m, V7X_LANES), F32),
                        pltpu.VMEM((d_ssm // V7X_LANES, m, V7X_LANES), F32),
                        pltpu.VMEM((n_half, m, n2), F32),
                        pltpu.VMEM((n_half, m, n2), BF16),
                        pltpu.VMEM((n_half, nb, n2), F32)],
        compiler_params=pltpu.CompilerParams(
            dimension_semantics=("arbitrary",), vmem_limit_bytes=_vmem_limit(est)),
        name="s5_scan_bwd" if reverse else "s5_scan_fwd",
    )(u, a, wb, wc)


def _ssm_params(a_re, a_im, log_dt, b_re, b_im, c_re, c_im, nb):
    g, p = a_re.shape
    h = b_re.shape[-1]
    n_half = (g * h) // V7X_MXU_DIM
    gh = g // n_half
    dt = jnp.exp(log_dt)[:, None]
    mag = jnp.exp(dt * a_re)
    ab_re = mag * jnp.cos(dt * a_im)
    ab_im = mag * jnp.sin(dt * a_im)
    den = a_re * a_re + a_im * a_im
    nr, ni = ab_re - 1.0, ab_im
    k_re = (nr * a_re + ni * a_im) / den
    k_im = (ni * a_re - nr * a_im) / den
    bb_re = k_re[..., None] * b_re - k_im[..., None] * b_im
    bb_im = k_re[..., None] * b_im + k_im[..., None] * b_re
    eye = jnp.eye(gh, dtype=F32)

    def in_proj(bb):
        bb = bb.reshape(n_half, gh, p, h)
        w = jnp.einsum('ngph,gk->nghkp', bb, eye)
        return w.reshape(n_half, gh * h, gh * p)

    def out_proj(c):
        c = c.reshape(n_half, gh, h, p)
        w = jnp.einsum('nghp,gk->ngpkh', c, eye)
        return w.reshape(n_half, gh * p, gh * h)

    wb = jnp.concatenate([in_proj(bb_re), in_proj(bb_im)], axis=-1).astype(BF16)
    wc = jnp.concatenate([out_proj(c_re), out_proj(-c_im)], axis=1).astype(BF16)
    a = jnp.concatenate([ab_re.reshape(n_half, gh * p), ab_im.reshape(n_half, gh * p)], axis=-1)
    a = jnp.broadcast_to(a[:, None, :], (n_half, nb, 2 * gh * p))
    return a, wb, wc


def _mix_out_kernel(x_ref, at_ref, yf_ref, yb_ref, u_ref, d_ref, wglu_ref, bglu_ref, gs_ref,
                    wo_ref, o_ref):
    d_attn = at_ref.shape[-1]
    y = yf_ref[0] + yb_ref[0] + d_ref[...] * u_ref[0]
    z = jax.nn.gelu(y)
    gate = jnp.dot(z.astype(BF16), wglu_ref[...], preferred_element_type=F32) + bglu_ref[...]
    s = z * jax.nn.sigmoid(gate)
    s_n = _rms(s, gs_ref[...]).astype(BF16)
    mixed = (jnp.dot(at_ref[0], wo_ref[:d_attn, :], preferred_element_type=F32)
             + jnp.dot(s_n, wo_ref[d_attn:, :], preferred_element_type=F32))
    o_ref[0] = x_ref[0] + mixed


def _mix_out(x3, attn_n, y_f, y_b, u, params, layer, *, tm=512):
    b, l, dm = x3.shape
    d_attn = attn_n.shape[-1]
    d_ssm = u.shape[-1]
    xrow = pl.BlockSpec((1, tm, dm), lambda bi, i: (bi, i, 0))
    srow = pl.BlockSpec((1, tm, d_ssm), lambda bi, i: (bi, i, 0))
    est = 2 * (params[1][0].size + params[4][0].size) + 4 * tm * dm * 4 + 12 * tm * d_ssm * 4
    return pl.pallas_call(
        _mix_out_kernel,
        out_shape=jax.ShapeDtypeStruct((b, l, dm), F32),
        grid=(b, l // tm),
        in_specs=[xrow,
                  pl.BlockSpec((1, tm, d_attn), lambda bi, i: (bi, i, 0)),
                  srow, srow, srow] + [_layer_spec(p, layer) for p in params],
        out_specs=xrow,
        compiler_params=pltpu.CompilerParams(
            dimension_semantics=("arbitrary", "arbitrary"), vmem_limit_bytes=_vmem_limit(est)),
        name="mix_out",
    )(x3, attn_n, y_f, y_b, u, *params)


def _rows(v):
    return v.astype(F32)[:, None, :]


def _ffn_params(norm, w_gate, w_up, w_down):
    return _rows(norm), w_gate.astype(BF16), w_up.astype(BF16), (0.5 * w_down).astype(BF16)


def kernel(x, rel_bias_table, ffn1_norm, ffn1_w_gate, ffn1_w_up, ffn1_w_down, mix_norm, w_in, attn_sink, ssm_a_re, ssm_a_im, ssm_log_dt, ssm_b_re, ssm_b_im, ssm_c_re, ssm_c_im, ssm_d, ssm_w_glu, ssm_b_glu, attn_out_norm, ssm_out_norm, w_out, ffn2_norm, ffn2_w_gate, ffn2_w_up, ffn2_w_down, final_norm):
    b, l, dm = x.shape
    depth = w_in.shape[0]
    d_attn = N_HEADS * HEAD_DIM
    d_kv2 = 2 * N_KV_HEADS * HEAD_DIM
    assert b == V7X_SUBLANES, "the scan keeps one batch sequence per sublane"

    bias_g = _group_bias(_rel_bias(rel_bias_table.astype(F32)))
    ffn1 = _ffn_params(ffn1_norm, ffn1_w_gate, ffn1_w_up, ffn1_w_down)
    ffn2 = _ffn_params(ffn2_norm, ffn2_w_gate, ffn2_w_up, ffn2_w_down)
    mix_g, w_in_b = _rows(mix_norm), w_in.astype(BF16)
    sink, attn_g = attn_sink.astype(F32), _rows(attn_out_norm)
    ssm = jax.vmap(jax.vmap(functools.partial(_ssm_params, nb=b)))(
        ssm_a_re, ssm_a_im, ssm_log_dt, ssm_b_re, ssm_b_im, ssm_c_re, ssm_c_im)
    out_p = (_rows(ssm_d), ssm_w_glu.astype(BF16), _rows(ssm_b_glu), _rows(ssm_out_norm), w_out.astype(BF16))
    final_g = final_norm.astype(F32).reshape(1, -1)

    for i in range(depth):
        x = _ffn(x.reshape(b * l, dm), ffn1, i).reshape(b, l, dm)
        q, kv, u = _proj(x, mix_g, w_in_b, i, d_attn=d_attn, d_kv2=d_kv2)
        attn_n = _attention(q, kv, bias_g, sink, attn_g, i)
        y_f = _ssm_scan(u, *ssm, i, reverse=False)
        y_b = _ssm_scan(u, *ssm, i, reverse=True)
        x = _mix_out(x, attn_n, y_f, y_b, u, out_p, i)
        x = _ffn(x.reshape(b * l, dm), ffn2, i, final_g if i == depth - 1 else None).reshape(b, l, dm)
    return x
```

```python
import functools
import math

import jax
import jax.numpy as jnp
from jax import lax
from jax.experimental import pallas as pl
from jax.experimental.pallas import tpu as pltpu

EPS = 1e-6
NEG_INF = -1e30

N_HEADS = 8
N_KV_HEADS = 2
HEAD_DIM = 64
SSM_GROUP = 16
SSM_STATE = 64
ATT_BLOCK = 128
N_BUCKETS = 32
MAX_DISTANCE = 128

V7X_SUBLANES = 8
V7X_LANES = 128
V7X_MXU_DIM = 256
V7X_VMEM_BYTES = 64 * 1024 * 1024

F32 = jnp.float32
BF16 = jnp.bfloat16


def _vmem_limit(nbytes):
    return int(min(nbytes + (8 << 20), V7X_VMEM_BYTES - (6 << 20)))


def _rms(x, g):
    return x * lax.rsqrt(jnp.mean(x * x, axis=-1, keepdims=True) + EPS) * g


def _const_spec(shape):
    nd = len(shape)
    return pl.BlockSpec(shape, lambda *_: (0,) * nd, pipeline_mode=pl.Buffered(1))


def _layer_spec(arr, *idx):
    rest = arr.shape[len(idx):]
    return pl.BlockSpec((None,) * len(idx) + rest, lambda *_: idx + (0,) * len(rest),
                        pipeline_mode=pl.Buffered(1))


def _ffn_kernel(x_ref, g_ref, wg_ref, wu_ref, wd_ref, *rest, fc, final):
    if final:
        fg_ref, o_ref, h_scr, gu_scr = rest
    else:
        o_ref, h_scr, gu_scr = rest
    n_chunks = wd_ref.shape[0] // fc
    x = x_ref[...]
    h_scr[...] = _rms(x, g_ref[...]).astype(BF16)
    o_ref[...] = x

    def gate_up(c):
        cols = slice(c * fc, (c + 1) * fc)
        gu_scr[c % 2, :, :fc] = jnp.dot(h_scr[...], wg_ref[:, cols], preferred_element_type=F32)
        gu_scr[c % 2, :, fc:] = jnp.dot(h_scr[...], wu_ref[:, cols], preferred_element_type=F32)

    def down(c):
        gu = gu_scr[c % 2]
        gate, up = gu[:, :fc], gu[:, fc:]
        act = (gate * jax.nn.sigmoid(gate) * up).astype(BF16)
        o_ref[...] += jnp.dot(act, wd_ref[c * fc:(c + 1) * fc, :], preferred_element_type=F32)

    gate_up(0)
    for c in range(n_chunks):
        if c + 1 < n_chunks:
            gate_up(c + 1)
        down(c)
    if final:
        o_ref[...] = _rms(o_ref[...], fg_ref[...])


def _ffn(x2, params, layer, final_g=None, *, tm=512, fc=V7X_MXU_DIM):
    t, d = x2.shape
    final = final_g is not None
    row = pl.BlockSpec((tm, d), lambda i: (i, 0))
    in_specs = [row] + [_layer_spec(p, layer) for p in params]
    args = [x2, *params]
    if final:
        in_specs.append(_const_spec((1, d)))
        args.append(final_g)
    est = 2 * sum(p[0].size for p in params[1:]) + 4 * tm * d * 4 + tm * d * 2 + 4 * tm * fc * 4
    return pl.pallas_call(
        functools.partial(_ffn_kernel, fc=fc, final=final),
        out_shape=jax.ShapeDtypeStruct((t, d), F32),
        grid=(t // tm,),
        in_specs=in_specs,
        out_specs=row,
        scratch_shapes=[pltpu.VMEM((tm, d), BF16), pltpu.VMEM((2, tm, 2 * fc), F32)],
        compiler_params=pltpu.CompilerParams(
            dimension_semantics=("arbitrary",), vmem_limit_bytes=_vmem_limit(est)),
        name="ffn",
    )(*args)


def _proj_kernel(x_ref, g_ref, w_ref, q_ref, kv_ref, u_ref, *, d_attn, d_kv2):
    h = _rms(x_ref[0], g_ref[...]).astype(BF16)
    p = jnp.dot(h, w_ref[...], preferred_element_type=F32)
    q_ref[0] = (p[:, :d_attn] * (HEAD_DIM ** -0.5)).astype(BF16)
    kv_ref[0] = p[:, d_attn:d_attn + d_kv2].astype(BF16)
    u_ref[0] = p[:, d_attn + d_kv2:]


def _proj(x3, g, w_in, layer, *, d_attn, d_kv2, tm=512):
    b, l, d = x3.shape
    d_in = w_in.shape[-1]
    d_ssm = d_in - d_attn - d_kv2
    est = 2 * w_in[0].size + 2 * tm * d * 4 + 3 * tm * d_in * 4
    return pl.pallas_call(
        functools.partial(_proj_kernel, d_attn=d_attn, d_kv2=d_kv2),
        out_shape=(
            jax.ShapeDtypeStruct((b, l, d_attn), BF16),
            jax.ShapeDtypeStruct((b, l, d_kv2), BF16),
            jax.ShapeDtypeStruct((b, l, d_ssm), F32),
        ),
        grid=(b, l // tm),
        in_specs=[pl.BlockSpec((1, tm, d), lambda bi, i: (bi, i, 0)),
                  _layer_spec(g, layer), _layer_spec(w_in, layer)],
        out_specs=(
            pl.BlockSpec((1, tm, d_attn), lambda bi, i: (bi, i, 0)),
            pl.BlockSpec((1, tm, d_kv2), lambda bi, i: (bi, i, 0)),
            pl.BlockSpec((1, tm, d_ssm), lambda bi, i: (bi, i, 0)),
        ),
        compiler_params=pltpu.CompilerParams(
            dimension_semantics=("arbitrary", "arbitrary"), vmem_limit_bytes=_vmem_limit(est)),
        name="mix_proj",
    )(x3, g, w_in)


def _bias_kernel(table_ref, idx_ref, band_ref, o_ref):
    idx = idx_ref[...]
    band = band_ref[...] > 0
    for h in range(N_HEADS):
        acc = jnp.zeros(idx.shape, F32)
        for bkt in range(N_BUCKETS):
            acc = jnp.where(idx == bkt, table_ref[bkt, h], acc)
        o_ref[h] = jnp.where(band, acc, NEG_INF)


def _t5_bucket(rel):
    half = N_BUCKETS // 2
    max_exact = half // 2
    ret = jnp.where(rel > 0, half, 0)
    n = jnp.abs(rel)
    nf = jnp.maximum(n, 1).astype(F32)
    large = max_exact + (jnp.log(nf / max_exact) / math.log(MAX_DISTANCE / max_exact)
                         * (half - max_exact)).astype(jnp.int32)
    large = jnp.minimum(large, half - 1)
    return ret + jnp.where(n < max_exact, n, large)


def _rel_bias(table):
    rel = (jnp.arange(3 * ATT_BLOCK)[None, :] - ATT_BLOCK) - jnp.arange(ATT_BLOCK)[:, None]
    idx = _t5_bucket(rel).astype(jnp.int32)
    band = (jnp.abs(rel) <= ATT_BLOCK).astype(jnp.int32)
    return pl.pallas_call(
        _bias_kernel,
        out_shape=jax.ShapeDtypeStruct((N_HEADS, ATT_BLOCK, 3 * ATT_BLOCK), F32),
        in_specs=[pl.BlockSpec(memory_space=pltpu.SMEM),
                  pl.BlockSpec(memory_space=pltpu.VMEM),
                  pl.BlockSpec(memory_space=pltpu.VMEM)],
        out_specs=pl.BlockSpec(memory_space=pltpu.VMEM),
        name="rel_bias",
    )(table, idx, band)


def _attn_kernel(sink_ref, q_ref, kvp_ref, kvm_ref, kvn_ref, bias_ref, g_ref, o_ref,
                 kd_scr, ve_scr, vo_scr, s_scr, *, nq, layer):
    i = pl.program_id(1)
    n_i = pl.num_programs(1)
    blk = ATT_BLOCK
    dh = HEAD_DIM
    dk2 = N_KV_HEADS * dh
    grp = N_HEADS // N_KV_HEADS
    assert dk2 == V7X_LANES and grp == 4
    lane = lax.broadcasted_iota(jnp.int32, (1, dk2), 1)
    m_lo = (lane < dh).astype(BF16)
    m_hi = (lane >= dh).astype(BF16)

    kv = jnp.concatenate([kvp_ref[0], kvm_ref[0], kvn_ref[0]], axis=0)
    rows = kv.shape[0]
    k_row, v_row = kv[:, :dk2], kv[:, dk2:]
    k_swap = jnp.concatenate([k_row[:, dh:], k_row[:, :dh]], axis=-1)
    v_swap = jnp.concatenate([v_row[:, dh:], v_row[:, :dh]], axis=-1)
    ones_lo = jnp.broadcast_to(m_lo, (rows, dk2))
    ones_hi = jnp.broadcast_to(m_hi, (rows, dk2))
    kd_scr[0] = k_row * m_lo + k_swap * m_hi
    kd_scr[1] = k_swap * m_lo + k_row * m_hi
    ve_scr[0] = jnp.concatenate([v_row * m_lo, ones_lo], axis=-1)
    vo_scr[0] = jnp.concatenate([v_swap * m_hi, ones_hi], axis=-1)
    ve_scr[1] = jnp.concatenate([v_swap * m_lo, ones_lo], axis=-1)
    vo_scr[1] = jnp.concatenate([v_row * m_hi, ones_hi], axis=-1)

    col = lax.broadcasted_iota(jnp.int32, (1, 3 * blk), 1)
    edge_first = jnp.where(col < blk, jnp.where(i == 0, NEG_INF, 0.0), 0.0)
    edge_last = jnp.where(col >= 2 * blk, jnp.where(i == n_i - 1, NEG_INF, 0.0), 0.0)
    lane_lo = lane < dh

    def scores(jb, kh):
        qs = []
        for msk in (m_lo, m_hi):
            for pr in range(grp // 2):
                c0 = kh * grp * dh + pr * dk2
                qs.append(q_ref[0, jb * blk:(jb + 1) * blk, c0:c0 + dk2] * msk)
        q_st = jnp.concatenate(qs, axis=0)
        s = lax.dot_general(q_st, kd_scr[kh, jb * blk:(jb + 3) * blk, :], (((1,), (1,)), ((), ())),
                            preferred_element_type=F32)
        s = s + bias_ref[kh]
        if jb == 0:
            s = s + edge_first
        if jb == nq - 1:
            s = s + edge_last
        return s

    tiles = [(jb, kh) for jb in range(nq) for kh in range(N_KV_HEADS)]
    s_scr[0] = scores(*tiles[0])
    pieces = []
    for n, (jb, kh) in enumerate(tiles):
        win = slice(jb * blk, (jb + 3) * blk)
        if n + 1 < len(tiles):
            s_scr[(n + 1) % 2] = scores(*tiles[n + 1])
        s = s_scr[n % 2]
        s_max = jnp.maximum(jnp.maximum(s[:, :blk], s[:, blk:2 * blk]), s[:, 2 * blk:])
        p, t = [], []
        for r, hh in enumerate((0, 2, 1, 3)):
            rs = slice(r * blk, (r + 1) * blk)
            sink = sink_ref[layer, kh * grp + hh]
            m = jnp.maximum(jnp.max(s_max[rs], axis=-1, keepdims=True), sink)
            p.append(jnp.exp(s[rs] - m).astype(BF16))
            t.append(jnp.exp(sink - m))
        for pr in range(grp // 2):
            ev, od = pr, grp // 2 + pr
            nd = (jnp.dot(p[ev], ve_scr[kh, win, :], preferred_element_type=F32)
                  + jnp.dot(p[od], vo_scr[kh, win, :], preferred_element_type=F32))
            den = nd[:, dk2:] + jnp.where(lane_lo, t[ev], t[od])
            pieces.append(nd[:, :dk2] / den)
        if kh == N_KV_HEADS - 1:
            o = jnp.concatenate(pieces, axis=-1)
            o_ref[0, jb * blk:(jb + 1) * blk, :] = _rms(o, g_ref[...]).astype(BF16)
            pieces = []


def _group_bias(bias):
    grp = N_HEADS // N_KV_HEADS
    bias_g = bias.reshape(N_KV_HEADS, grp, ATT_BLOCK, 3 * ATT_BLOCK)[:, jnp.array([0, 2, 1, 3])]
    return bias_g.reshape(N_KV_HEADS, grp * ATT_BLOCK, 3 * ATT_BLOCK)


def _attention(q, kv, bias_g, sink, g, layer, *, nq=4):
    b, l, d_attn = q.shape
    d_kv2 = kv.shape[-1]
    blk = ATT_BLOCK
    tq = nq * blk
    nb = l // blk
    grp = N_HEADS // N_KV_HEADS
    rows = tq + 2 * blk
    est = (bias_g.size * 4 + 6 * tq * (d_attn + d_kv2) * 2 + N_KV_HEADS * rows * 5 * V7X_LANES * 2
           + 6 * grp * blk * 3 * blk * 4)
    return pl.pallas_call(
        functools.partial(_attn_kernel, nq=nq, layer=layer),
        out_shape=jax.ShapeDtypeStruct((b, l, d_attn), BF16),
        grid=(b, l // tq),
        in_specs=[
            pl.BlockSpec(memory_space=pltpu.SMEM),
            pl.BlockSpec((1, tq, d_attn), lambda bi, i: (bi, i, 0)),
            pl.BlockSpec((1, blk, d_kv2), lambda bi, i: (bi, jnp.maximum(i * nq - 1, 0), 0)),
            pl.BlockSpec((1, tq, d_kv2), lambda bi, i: (bi, i, 0)),
            pl.BlockSpec((1, blk, d_kv2), lambda bi, i: (bi, jnp.minimum((i + 1) * nq, nb - 1), 0)),
            _const_spec(bias_g.shape),
            _layer_spec(g, layer),
        ],
        out_specs=pl.BlockSpec((1, tq, d_attn), lambda bi, i: (bi, i, 0)),
        scratch_shapes=[pltpu.VMEM((N_KV_HEADS, rows, V7X_LANES), BF16),
                        pltpu.VMEM((N_KV_HEADS, rows, 2 * V7X_LANES), BF16),
                        pltpu.VMEM((N_KV_HEADS, rows, 2 * V7X_LANES), BF16),
                        pltpu.VMEM((2, grp * blk, 3 * blk), F32)],
        compiler_params=pltpu.CompilerParams(
            dimension_semantics=("arbitrary", "arbitrary"), vmem_limit_bytes=_vmem_limit(est)),
        name="band_attn",
    )(sink, q, kv, kv, kv, bias_g, g)


def _ssm_kernel(u_ref, a_ref, wb_ref, wc_ref, y_ref, ut_scr, yt_scr, x_scr, s_scr, st_scr,
                *, tc, nb, reverse, lane_tile):
    n_half, _, n2 = a_ref.shape
    nre = n2 // 2
    kh = wb_ref.shape[1]
    rows2 = 2 * nb
    n_pairs = tc // 2
    n_slab = ut_scr.shape[0]
    slab_per_half = kh // V7X_LANES

    @pl.when(pl.program_id(0) == 0)
    def _():
        st_scr[...] = jnp.zeros_like(st_scr)

    for bi in range(nb):
        for sl in range(n_slab):
            ut_scr.at[sl][pl.ds(bi, tc, stride=nb), :] = u_ref[bi, :, sl * V7X_LANES:(sl + 1) * V7X_LANES]

    for hf in range(n_half):
        u_h = jnp.concatenate([ut_scr[hf * slab_per_half + k] for k in range(slab_per_half)], axis=-1)
        x_scr[hf] = jnp.dot(u_h.astype(BF16), wb_ref[hf], preferred_element_type=F32)
    for hf in range(n_half):
        for lt in range(nre // lane_tile):
            re_sl = slice(lt * lane_tile, (lt + 1) * lane_tile)
            im_sl = slice(nre + lt * lane_tile, nre + (lt + 1) * lane_tile)
            a_re = a_ref[hf, :, re_sl]
            a_im = a_ref[hf, :, im_sl]

            def step(s_re, s_im, x_re, x_im):
                n_re = a_re * s_re - a_im * s_im + x_re
                n_im = a_re * s_im + a_im * s_re + x_im
                return n_re, n_im

            s_re, s_im = st_scr[hf, :, re_sl], st_scr[hf, :, im_sl]
            for j in range(n_pairs):
                r0 = ((n_pairs - 1 - j) if reverse else j) * rows2
                x_re = x_scr[hf, r0:r0 + rows2, re_sl]
                x_im = x_scr[hf, r0:r0 + rows2, im_sl]
                first, second = (slice(nb, rows2), slice(0, nb)) if reverse else (slice(0, nb), slice(nb, rows2))
                s1_re, s1_im = step(s_re, s_im, x_re[first], x_im[first])
                s_re, s_im = step(s1_re, s1_im, x_re[second], x_im[second])
                lo_re, hi_re = (s_re, s1_re) if reverse else (s1_re, s_re)
                lo_im, hi_im = (s_im, s1_im) if reverse else (s1_im, s_im)
                s_scr[hf, r0:r0 + rows2, re_sl] = jnp.concatenate([lo_re, hi_re], axis=0).astype(BF16)
                s_scr[hf, r0:r0 + rows2, im_sl] = jnp.concatenate([lo_im, hi_im], axis=0).astype(BF16)
            st_scr[hf, :, re_sl] = s_re
            st_scr[hf, :, im_sl] = s_im
        mh = (tc * nb) // 2
        for r in range(2):
            y_h = jnp.dot(s_scr[hf, r * mh:(r + 1) * mh, :], wc_ref[hf], preferred_element_type=F32)
            for k in range(slab_per_half):
                yt_scr[hf * slab_per_half + k, r * mh:(r + 1) * mh, :] = y_h[:, k * V7X_LANES:(k + 1) * V7X_LANES]

    for bi in range(nb):
        for sl in range(n_slab):
            y_ref[bi, :, sl * V7X_LANES:(sl + 1) * V7X_LANES] = yt_scr.at[sl][pl.ds(bi, tc, stride=nb), :]


def _ssm_scan(u, a, wb, wc, layer, *, reverse, tc=64, lane_tile=512):
    nb, l, d_ssm = u.shape
    n_half, kh, n2 = wb.shape[2:]
    m = tc * nb
    n_t = l // tc
    di = int(reverse)
    idx = (lambda i: (0, n_t - 1 - i, 0)) if reverse else (lambda i: (0, i, 0))
    est = (6 * m * d_ssm * 4 + n_half * m * n2 * 6
           + 2 * (wb[0, 0].size + wc[0, 0].size) + 4 * a[0, 0].size)
    return pl.pallas_call(
        functools.partial(_ssm_kernel, tc=tc, nb=nb, reverse=reverse, lane_tile=lane_tile),
        out_shape=jax.ShapeDtypeStruct((nb, l, d_ssm), F32),
        grid=(n_t,),
        in_specs=[pl.BlockSpec((nb, tc, d_ssm), idx),
                  _layer_spec(a, layer, di), _layer_spec(wb, layer, di), _layer_spec(wc, layer, di)],
        out_specs=pl.BlockSpec((nb, tc, d_ssm), idx),
        scratch_shapes=[pltpu.VMEM((d_ssm // V7X_LANES, m, V7X_LANES), F32),
                        pltpu.VMEM((d_ssm // V7X_LANES, m, V7X_LANES), F32),
                        pltpu.VMEM((n_half, m, n2), F32),
                        pltpu.VMEM((n_half, m, n2), BF16),
                        pltpu.VMEM((n_half, nb, n2), F32)],
        compiler_params=pltpu.CompilerParams(
            dimension_semantics=("arbitrary",), vmem_limit_bytes=_vmem_limit(est)),
        name="s5_scan_bwd" if reverse else "s5_scan_fwd",
    )(u, a, wb, wc)


def _ssm_params(a_re, a_im, log_dt, b_re, b_im, c_re, c_im, nb):
    g, p = a_re.shape
    h = b_re.shape[-1]
    n_half = (g * h) // V7X_MXU_DIM
    gh = g // n_half
    dt = jnp.exp(log_dt)[:, None]
    mag = jnp.exp(dt * a_re)
    ab_re = mag * jnp.cos(dt * a_im)
    ab_im = mag * jnp.sin(dt * a_im)
    den = a_re * a_re + a_im * a_im
    nr, ni = ab_re - 1.0, ab_im
    k_re = (nr * a_re + ni * a_im) / den
    k_im = (ni * a_re - nr * a_im) / den
    bb_re = k_re[..., None] * b_re - k_im[..., None] * b_im
    bb_im = k_re[..., None] * b_im + k_im[..., None] * b_re
    grp_of = lambda n_rows_per_group: jnp.arange(gh * n_rows_per_group) // n_rows_per_group
    in_mask = grp_of(h)[:, None] == grp_of(p)[None, :]
    out_mask = in_mask.T

    def in_proj(bb):
        rows = jnp.swapaxes(bb, 1, 2).reshape(n_half, gh * h, p)
        return jnp.where(in_mask, jnp.tile(rows, (1, 1, gh)), 0.0)

    def out_proj(c):
        rows = jnp.swapaxes(c, 1, 2).reshape(n_half, gh * p, h)
        return jnp.where(out_mask, jnp.tile(rows, (1, 1, gh)), 0.0)

    wb = jnp.concatenate([in_proj(bb_re), in_proj(bb_im)], axis=-1).astype(BF16)
    wc = jnp.concatenate([out_proj(c_re), out_proj(-c_im)], axis=1).astype(BF16)
    a = jnp.concatenate([ab_re.reshape(n_half, gh * p), ab_im.reshape(n_half, gh * p)], axis=-1)
    a = jnp.broadcast_to(a[:, None, :], (n_half, nb, 2 * gh * p))
    return a, wb, wc


def _mix_out_kernel(x_ref, at_ref, yf_ref, yb_ref, u_ref, d_ref, wglu_ref, bglu_ref, gs_ref,
                    wo_ref, o_ref, z_scr, gate_scr):
    d_attn = at_ref.shape[-1]
    n_sub, sub, _ = z_scr.shape
    rows = [slice(r * sub, (r + 1) * sub) for r in range(n_sub)]

    def stage_z(r):
        y = yf_ref[0, rows[r], :] + yb_ref[0, rows[r], :] + d_ref[...] * u_ref[0, rows[r], :]
        z_scr[r] = jax.nn.gelu(y)

    def stage_gate(r):
        gate_scr[r] = jnp.dot(z_scr[r].astype(BF16), wglu_ref[...], preferred_element_type=F32)
        o_ref[0, rows[r], :] = x_ref[0, rows[r], :] + jnp.dot(at_ref[0, rows[r], :], wo_ref[:d_attn, :],
                                                             preferred_element_type=F32)

    def stage_out(r):
        s = z_scr[r] * jax.nn.sigmoid(gate_scr[r] + bglu_ref[...])
        s_n = _rms(s, gs_ref[...]).astype(BF16)
        o_ref[0, rows[r], :] += jnp.dot(s_n, wo_ref[d_attn:, :], preferred_element_type=F32)

    stage_z(0)
    for r in range(n_sub):
        stage_gate(r)
        if r + 1 < n_sub:
            stage_z(r + 1)
        if r > 0:
            stage_out(r - 1)
    stage_out(n_sub - 1)


def _mix_out(x3, attn_n, y_f, y_b, u, params, layer, *, tm=512, n_sub=2):
    b, l, dm = x3.shape
    d_attn = attn_n.shape[-1]
    d_ssm = u.shape[-1]
    xrow = pl.BlockSpec((1, tm, dm), lambda bi, i: (bi, i, 0))
    srow = pl.BlockSpec((1, tm, d_ssm), lambda bi, i: (bi, i, 0))
    est = 2 * (params[1][0].size + params[4][0].size) + 4 * tm * dm * 4 + 12 * tm * d_ssm * 4
    return pl.pallas_call(
        _mix_out_kernel,
        out_shape=jax.ShapeDtypeStruct((b, l, dm), F32),
        grid=(b, l // tm),
        in_specs=[xrow,
                  pl.BlockSpec((1, tm, d_attn), lambda bi, i: (bi, i, 0)),
                  srow, srow, srow] + [_layer_spec(p, layer) for p in params],
        out_specs=xrow,
        scratch_shapes=[pltpu.VMEM((n_sub, tm // n_sub, d_ssm), F32),
                        pltpu.VMEM((n_sub, tm // n_sub, d_ssm), F32)],
        compiler_params=pltpu.CompilerParams(
            dimension_semantics=("arbitrary", "arbitrary"), vmem_limit_bytes=_vmem_limit(est)),
        name="mix_out",
    )(x3, attn_n, y_f, y_b, u, *params)


def _rows(v):
    return v.astype(F32)[:, None, :]


def _ffn_params(norm, w_gate, w_up, w_down):
    return _rows(norm), w_gate.astype(BF16), w_up.astype(BF16), (0.5 * w_down).astype(BF16)


def kernel(x, rel_bias_table, ffn1_norm, ffn1_w_gate, ffn1_w_up, ffn1_w_down, mix_norm, w_in, attn_sink, ssm_a_re, ssm_a_im, ssm_log_dt, ssm_b_re, ssm_b_im, ssm_c_re, ssm_c_im, ssm_d, ssm_w_glu, ssm_b_glu, attn_out_norm, ssm_out_norm, w_out, ffn2_norm, ffn2_w_gate, ffn2_w_up, ffn2_w_down, final_norm):
    b, l, dm = x.shape
    depth = w_in.shape[0]
    d_attn = N_HEADS * HEAD_DIM
    d_kv2 = 2 * N_KV_HEADS * HEAD_DIM
    assert b == V7X_SUBLANES, "the scan keeps one batch sequence per sublane"

    bias_g = _group_bias(_rel_bias(rel_bias_table.astype(F32)))
    ffn1 = _ffn_params(ffn1_norm, ffn1_w_gate, ffn1_w_up, ffn1_w_down)
    ffn2 = _ffn_params(ffn2_norm, ffn2_w_gate, ffn2_w_up, ffn2_w_down)
    mix_g, w_in_b = _rows(mix_norm), w_in.astype(BF16)
    sink, attn_g = attn_sink.astype(F32), _rows(attn_out_norm)
    ssm = jax.vmap(jax.vmap(functools.partial(_ssm_params, nb=b)))(
        ssm_a_re, ssm_a_im, ssm_log_dt, ssm_b_re, ssm_b_im, ssm_c_re, ssm_c_im)
    out_p = (_rows(ssm_d), ssm_w_glu.astype(BF16), _rows(ssm_b_glu), _rows(ssm_out_norm), w_out.astype(BF16))
    final_g = final_norm.astype(F32).reshape(1, -1)

    for i in range(depth):
        x = _ffn(x.reshape(b * l, dm), ffn1, i).reshape(b, l, dm)
        q, kv, u = _proj(x, mix_g, w_in_b, i, d_attn=d_attn, d_kv2=d_kv2)
        attn_n = _attention(q, kv, bias_g, sink, attn_g, i)
        y_f = _ssm_scan(u, *ssm, i, reverse=False)
        y_b = _ssm_scan(u, *ssm, i, reverse=True)
        x = _mix_out(x, attn_n, y_f, y_b, u, out_p, i)
        x = _ffn(x.reshape(b * l, dm), ffn2, i, final_g if i == depth - 1 else None).reshape(b, l, dm)
    return x
```

```python
import functools
import math

import jax
import jax.numpy as jnp
from jax import lax
from jax.experimental import pallas as pl
from jax.experimental.pallas import tpu as pltpu

EPS = 1e-6
NEG_INF = -1e30

N_HEADS = 8
N_KV_HEADS = 2
HEAD_DIM = 64
SSM_GROUP = 16
SSM_STATE = 64
ATT_BLOCK = 128
N_BUCKETS = 32
MAX_DISTANCE = 128

V7X_SUBLANES = 8
V7X_LANES = 128
V7X_MXU_DIM = 256
V7X_VMEM_BYTES = 64 * 1024 * 1024

F32 = jnp.float32
BF16 = jnp.bfloat16


def _vmem_limit(nbytes):
    return int(min(nbytes + (8 << 20), V7X_VMEM_BYTES - (6 << 20)))


def _rms(x, g):
    return x * lax.rsqrt(jnp.mean(x * x, axis=-1, keepdims=True) + EPS) * g


def _const_spec(shape):
    nd = len(shape)
    return pl.BlockSpec(shape, lambda *_: (0,) * nd, pipeline_mode=pl.Buffered(1))


def _layer_spec(arr, *idx):
    rest = arr.shape[len(idx):]
    return pl.BlockSpec((None,) * len(idx) + rest, lambda *_: idx + (0,) * len(rest),
                        pipeline_mode=pl.Buffered(1))


def _ffn_kernel(x_ref, g_ref, wg_ref, wu_ref, wd_ref, *rest, fc, final):
    if final:
        fg_ref, o_ref, h_scr, gu_scr = rest
    else:
        o_ref, h_scr, gu_scr = rest
    n_chunks = wd_ref.shape[0] // fc
    x = x_ref[...]
    h_scr[...] = _rms(x, g_ref[...]).astype(BF16)
    o_ref[...] = x

    def gate_up(c):
        cols = slice(c * fc, (c + 1) * fc)
        gu_scr[c % 2, :, :fc] = jnp.dot(h_scr[...], wg_ref[:, cols], preferred_element_type=F32)
        gu_scr[c % 2, :, fc:] = jnp.dot(h_scr[...], wu_ref[:, cols], preferred_element_type=F32)

    def down(c):
        gu = gu_scr[c % 2]
        gate, up = gu[:, :fc], gu[:, fc:]
        act = (gate * jax.nn.sigmoid(gate) * up).astype(BF16)
        o_ref[...] += jnp.dot(act, wd_ref[c * fc:(c + 1) * fc, :], preferred_element_type=F32)

    gate_up(0)
    for c in range(n_chunks):
        if c + 1 < n_chunks:
            gate_up(c + 1)
        down(c)
    if final:
        o_ref[...] = _rms(o_ref[...], fg_ref[...])


def _ffn(x2, params, layer, final_g=None, *, tm=512, fc=V7X_MXU_DIM):
    t, d = x2.shape
    final = final_g is not None
    row = pl.BlockSpec((tm, d), lambda i: (i, 0))
    in_specs = [row] + [_layer_spec(p, layer) for p in params]
    args = [x2, *params]
    if final:
        in_specs.append(_const_spec((1, d)))
        args.append(final_g)
    est = 2 * sum(p[0].size for p in params[1:]) + 4 * tm * d * 4 + tm * d * 2 + 4 * tm * fc * 4
    return pl.pallas_call(
        functools.partial(_ffn_kernel, fc=fc, final=final),
        out_shape=jax.ShapeDtypeStruct((t, d), F32),
        grid=(t // tm,),
        in_specs=in_specs,
        out_specs=row,
        scratch_shapes=[pltpu.VMEM((tm, d), BF16), pltpu.VMEM((2, tm, 2 * fc), F32)],
        compiler_params=pltpu.CompilerParams(
            dimension_semantics=("arbitrary",), vmem_limit_bytes=_vmem_limit(est)),
        name="ffn",
    )(*args)


def _proj_kernel(x_ref, g_ref, w_ref, q_ref, kv_ref, u_ref, *, d_attn, d_kv2):
    h = _rms(x_ref[0], g_ref[...]).astype(BF16)
    p = jnp.dot(h, w_ref[...], preferred_element_type=F32)
    q_ref[0] = (p[:, :d_attn] * (HEAD_DIM ** -0.5)).astype(BF16)
    kv_ref[0] = p[:, d_attn:d_attn + d_kv2].astype(BF16)
    u_ref[0] = p[:, d_attn + d_kv2:]


def _proj(x3, g, w_in, layer, *, d_attn, d_kv2, tm=512):
    b, l, d = x3.shape
    d_in = w_in.shape[-1]
    d_ssm = d_in - d_attn - d_kv2
    est = 2 * w_in[0].size + 2 * tm * d * 4 + 3 * tm * d_in * 4
    return pl.pallas_call(
        functools.partial(_proj_kernel, d_attn=d_attn, d_kv2=d_kv2),
        out_shape=(
            jax.ShapeDtypeStruct((b, l, d_attn), BF16),
            jax.ShapeDtypeStruct((b, l, d_kv2), BF16),
            jax.ShapeDtypeStruct((b, l, d_ssm), F32),
        ),
        grid=(b, l // tm),
        in_specs=[pl.BlockSpec((1, tm, d), lambda bi, i: (bi, i, 0)),
                  _layer_spec(g, layer), _layer_spec(w_in, layer)],
        out_specs=(
            pl.BlockSpec((1, tm, d_attn), lambda bi, i: (bi, i, 0)),
            pl.BlockSpec((1, tm, d_kv2), lambda bi, i: (bi, i, 0)),
            pl.BlockSpec((1, tm, d_ssm), lambda bi, i: (bi, i, 0)),
        ),
        compiler_params=pltpu.CompilerParams(
            dimension_semantics=("arbitrary", "arbitrary"), vmem_limit_bytes=_vmem_limit(est)),
        name="mix_proj",
    )(x3, g, w_in)


def _bias_kernel(table_ref, idx_ref, band_ref, o_ref):
    idx = idx_ref[...]
    band = band_ref[...] > 0
    for h in range(N_HEADS):
        acc = jnp.zeros(idx.shape, F32)
        for bkt in range(N_BUCKETS):
            acc = jnp.where(idx == bkt, table_ref[bkt, h], acc)
        o_ref[h] = jnp.where(band, acc, NEG_INF)


def _t5_bucket(rel):
    half = N_BUCKETS // 2
    max_exact = half // 2
    ret = jnp.where(rel > 0, half, 0)
    n = jnp.abs(rel)
    nf = jnp.maximum(n, 1).astype(F32)
    large = max_exact + (jnp.log(nf / max_exact) / math.log(MAX_DISTANCE / max_exact)
                         * (half - max_exact)).astype(jnp.int32)
    large = jnp.minimum(large, half - 1)
    return ret + jnp.where(n < max_exact, n, large)


def _rel_bias(table):
    rel = (jnp.arange(3 * ATT_BLOCK)[None, :] - ATT_BLOCK) - jnp.arange(ATT_BLOCK)[:, None]
    idx = _t5_bucket(rel).astype(jnp.int32)
    band = (jnp.abs(rel) <= ATT_BLOCK).astype(jnp.int32)
    return pl.pallas_call(
        _bias_kernel,
        out_shape=jax.ShapeDtypeStruct((N_HEADS, ATT_BLOCK, 3 * ATT_BLOCK), F32),
        in_specs=[pl.BlockSpec(memory_space=pltpu.SMEM),
                  pl.BlockSpec(memory_space=pltpu.VMEM),
                  pl.BlockSpec(memory_space=pltpu.VMEM)],
        out_specs=pl.BlockSpec(memory_space=pltpu.VMEM),
        name="rel_bias",
    )(table, idx, band)


def _attn_kernel(sink_ref, q_ref, kvp_ref, kvm_ref, kvn_ref, bias_ref, g_ref, o_ref,
                 kd_scr, ve_scr, vo_scr, s_scr, *, nq, layer):
    i = pl.program_id(1)
    n_i = pl.num_programs(1)
    blk = ATT_BLOCK
    dh = HEAD_DIM
    dk2 = N_KV_HEADS * dh
    grp = N_HEADS // N_KV_HEADS
    assert dk2 == V7X_LANES and grp == 4
    lane = lax.broadcasted_iota(jnp.int32, (1, dk2), 1)
    m_lo = (lane < dh).astype(BF16)
    m_hi = (lane >= dh).astype(BF16)

    kv = jnp.concatenate([kvp_ref[0], kvm_ref[0], kvn_ref[0]], axis=0)
    rows = kv.shape[0]
    k_row, v_row = kv[:, :dk2], kv[:, dk2:]
    k_swap = jnp.concatenate([k_row[:, dh:], k_row[:, :dh]], axis=-1)
    v_swap = jnp.concatenate([v_row[:, dh:], v_row[:, :dh]], axis=-1)
    ones_lo = jnp.broadcast_to(m_lo, (rows, dk2))
    ones_hi = jnp.broadcast_to(m_hi, (rows, dk2))
    kd_scr[0] = k_row * m_lo + k_swap * m_hi
    kd_scr[1] = k_swap * m_lo + k_row * m_hi
    ve_scr[0] = jnp.concatenate([v_row * m_lo, ones_lo], axis=-1)
    vo_scr[0] = jnp.concatenate([v_swap * m_hi, ones_hi], axis=-1)
    ve_scr[1] = jnp.concatenate([v_swap * m_lo, ones_lo], axis=-1)
    vo_scr[1] = jnp.concatenate([v_row * m_hi, ones_hi], axis=-1)

    col = lax.broadcasted_iota(jnp.int32, (1, 3 * blk), 1)
    edge_first = jnp.where(col < blk, jnp.where(i == 0, NEG_INF, 0.0), 0.0)
    edge_last = jnp.where(col >= 2 * blk, jnp.where(i == n_i - 1, NEG_INF, 0.0), 0.0)
    lane_lo = lane < dh

    def scores(jb, kh):
        qs = []
        for msk in (m_lo, m_hi):
            for pr in range(grp // 2):
                c0 = kh * grp * dh + pr * dk2
                qs.append(q_ref[0, jb * blk:(jb + 1) * blk, c0:c0 + dk2] * msk)
        q_st = jnp.concatenate(qs, axis=0)
        s = lax.dot_general(q_st, kd_scr[kh, jb * blk:(jb + 3) * blk, :], (((1,), (1,)), ((), ())),
                            preferred_element_type=F32)
        s = s + bias_ref[kh]
        if jb == 0:
            s = s + edge_first
        if jb == nq - 1:
            s = s + edge_last
        return s

    tiles = [(jb, kh) for jb in range(nq) for kh in range(N_KV_HEADS)]
    s_scr[0] = scores(*tiles[0])
    pieces = []
    for n, (jb, kh) in enumerate(tiles):
        win = slice(jb * blk, (jb + 3) * blk)
        if n + 1 < len(tiles):
            s_scr[(n + 1) % 2] = scores(*tiles[n + 1])
        s = s_scr[n % 2]
        s_max = jnp.maximum(jnp.maximum(s[:, :blk], s[:, blk:2 * blk]), s[:, 2 * blk:])
        p, t = [], []
        for r, hh in enumerate((0, 2, 1, 3)):
            rs = slice(r * blk, (r + 1) * blk)
            sink = sink_ref[layer, kh * grp + hh]
            m = jnp.maximum(jnp.max(s_max[rs], axis=-1, keepdims=True), sink)
            p.append(jnp.exp(s[rs] - m).astype(BF16))
            t.append(jnp.exp(sink - m))
        for pr in range(grp // 2):
            ev, od = pr, grp // 2 + pr
            nd = (jnp.dot(p[ev], ve_scr[kh, win, :], preferred_element_type=F32)
                  + jnp.dot(p[od], vo_scr[kh, win, :], preferred_element_type=F32))
            den = nd[:, dk2:] + jnp.where(lane_lo, t[ev], t[od])
            pieces.append(nd[:, :dk2] / den)
        if kh == N_KV_HEADS - 1:
            o = jnp.concatenate(pieces, axis=-1)
            o_ref[0, jb * blk:(jb + 1) * blk, :] = _rms(o, g_ref[...]).astype(BF16)
            pieces = []


def _group_bias(bias):
    grp = N_HEADS // N_KV_HEADS
    bias_g = bias.reshape(N_KV_HEADS, grp, ATT_BLOCK, 3 * ATT_BLOCK)[:, jnp.array([0, 2, 1, 3])]
    return bias_g.reshape(N_KV_HEADS, grp * ATT_BLOCK, 3 * ATT_BLOCK)


def _attention(q, kv, bias_g, sink, g, layer, *, nq=4):
    b, l, d_attn = q.shape
    d_kv2 = kv.shape[-1]
    blk = ATT_BLOCK
    tq = nq * blk
    nb = l // blk
    grp = N_HEADS // N_KV_HEADS
    rows = tq + 2 * blk
    est = (bias_g.size * 4 + 6 * tq * (d_attn + d_kv2) * 2 + N_KV_HEADS * rows * 5 * V7X_LANES * 2
           + 6 * grp * blk * 3 * blk * 4)
    return pl.pallas_call(
        functools.partial(_attn_kernel, nq=nq, layer=layer),
        out_shape=jax.ShapeDtypeStruct((b, l, d_attn), BF16),
        grid=(b, l // tq),
        in_specs=[
            pl.BlockSpec(memory_space=pltpu.SMEM),
            pl.BlockSpec((1, tq, d_attn), lambda bi, i: (bi, i, 0)),
            pl.BlockSpec((1, blk, d_kv2), lambda bi, i: (bi, jnp.maximum(i * nq - 1, 0), 0)),
            pl.BlockSpec((1, tq, d_kv2), lambda bi, i: (bi, i, 0)),
            pl.BlockSpec((1, blk, d_kv2), lambda bi, i: (bi, jnp.minimum((i + 1) * nq, nb - 1), 0)),
            _const_spec(bias_g.shape),
            _layer_spec(g, layer),
        ],
        out_specs=pl.BlockSpec((1, tq, d_attn), lambda bi, i: (bi, i, 0)),
        scratch_shapes=[pltpu.VMEM((N_KV_HEADS, rows, V7X_LANES), BF16),
                        pltpu.VMEM((N_KV_HEADS, rows, 2 * V7X_LANES), BF16),
                        pltpu.VMEM((N_KV_HEADS, rows, 2 * V7X_LANES), BF16),
                        pltpu.VMEM((2, grp * blk, 3 * blk), F32)],
        compiler_params=pltpu.CompilerParams(
            dimension_semantics=("arbitrary", "arbitrary"), vmem_limit_bytes=_vmem_limit(est)),
        name="band_attn",
    )(sink, q, kv, kv, kv, bias_g, g)


def _ssm_kernel(u_ref, a_ref, wb_ref, wc_ref, add_ref, y_ref, ut_scr, yt_scr, x_scr, s_scr, st_scr,
                *, tc, nb, reverse, lane_tile):
    n_half, _, n2 = a_ref.shape
    nre = n2 // 2
    kh = wb_ref.shape[1]
    rows2 = 2 * nb
    n_pairs = tc // 2
    n_slab = ut_scr.shape[0]
    slab_per_half = kh // V7X_LANES

    @pl.when(pl.program_id(0) == 0)
    def _():
        st_scr[...] = jnp.zeros_like(st_scr)

    for bi in range(nb):
        for sl in range(n_slab):
            ut_scr.at[sl][pl.ds(bi, tc, stride=nb), :] = u_ref[bi, :, sl * V7X_LANES:(sl + 1) * V7X_LANES]

    for hf in range(n_half):
        u_h = jnp.concatenate([ut_scr[hf * slab_per_half + k] for k in range(slab_per_half)], axis=-1)
        x_scr[hf] = jnp.dot(u_h.astype(BF16), wb_ref[hf], preferred_element_type=F32)
    for hf in range(n_half):
        for lt in range(nre // lane_tile):
            re_sl = slice(lt * lane_tile, (lt + 1) * lane_tile)
            im_sl = slice(nre + lt * lane_tile, nre + (lt + 1) * lane_tile)
            a_re = a_ref[hf, :, re_sl]
            a_im = a_ref[hf, :, im_sl]

            def step(s_re, s_im, x_re, x_im):
                n_re = a_re * s_re - a_im * s_im + x_re
                n_im = a_re * s_im + a_im * s_re + x_im
                return n_re, n_im

            s_re, s_im = st_scr[hf, :, re_sl], st_scr[hf, :, im_sl]
            for j in range(n_pairs):
                r0 = ((n_pairs - 1 - j) if reverse else j) * rows2
                x_re = x_scr[hf, r0:r0 + rows2, re_sl]
                x_im = x_scr[hf, r0:r0 + rows2, im_sl]
                first, second = (slice(nb, rows2), slice(0, nb)) if reverse else (slice(0, nb), slice(nb, rows2))
                s1_re, s1_im = step(s_re, s_im, x_re[first], x_im[first])
                s_re, s_im = step(s1_re, s1_im, x_re[second], x_im[second])
                lo_re, hi_re = (s_re, s1_re) if reverse else (s1_re, s_re)
                lo_im, hi_im = (s_im, s1_im) if reverse else (s1_im, s_im)
                s_scr[hf, r0:r0 + rows2, re_sl] = jnp.concatenate([lo_re, hi_re], axis=0).astype(BF16)
                s_scr[hf, r0:r0 + rows2, im_sl] = jnp.concatenate([lo_im, hi_im], axis=0).astype(BF16)
            st_scr[hf, :, re_sl] = s_re
            st_scr[hf, :, im_sl] = s_im
        mh = (tc * nb) // 2
        for r in range(2):
            y_h = jnp.dot(s_scr[hf, r * mh:(r + 1) * mh, :], wc_ref[hf], preferred_element_type=F32)
            for k in range(slab_per_half):
                yt_scr[hf * slab_per_half + k, r * mh:(r + 1) * mh, :] = y_h[:, k * V7X_LANES:(k + 1) * V7X_LANES]

    for bi in range(nb):
        for sl in range(n_slab):
            lanes = slice(sl * V7X_LANES, (sl + 1) * V7X_LANES)
            y = yt_scr.at[sl][pl.ds(bi, tc, stride=nb), :]
            if reverse:
                y = y + add_ref[bi, :, lanes]
            else:
                y = y + add_ref[:, lanes] * u_ref[bi, :, lanes]
            y_ref[bi, :, lanes] = y


def _ssm_scan(u, add, a, wb, wc, layer, *, reverse, tc=64, lane_tile=512):
    nb, l, d_ssm = u.shape
    n_half, kh, n2 = wb.shape[2:]
    m = tc * nb
    n_t = l // tc
    di = int(reverse)
    idx = (lambda i: (0, n_t - 1 - i, 0)) if reverse else (lambda i: (0, i, 0))
    est = (6 * m * d_ssm * 4 + n_half * m * n2 * 6
           + 2 * (wb[0, 0].size + wc[0, 0].size) + 4 * a[0, 0].size)
    return pl.pallas_call(
        functools.partial(_ssm_kernel, tc=tc, nb=nb, reverse=reverse, lane_tile=lane_tile),
        out_shape=jax.ShapeDtypeStruct((nb, l, d_ssm), F32),
        grid=(n_t,),
        in_specs=[pl.BlockSpec((nb, tc, d_ssm), idx),
                  _layer_spec(a, layer, di), _layer_spec(wb, layer, di), _layer_spec(wc, layer, di),
                  pl.BlockSpec((nb, tc, d_ssm), idx) if reverse else _layer_spec(add, layer)],
        out_specs=pl.BlockSpec((nb, tc, d_ssm), idx),
        scratch_shapes=[pltpu.VMEM((d_ssm // V7X_LANES, m, V7X_LANES), F32),
                        pltpu.VMEM((d_ssm // V7X_LANES, m, V7X_LANES), F32),
                        pltpu.VMEM((n_half, m, n2), F32),
                        pltpu.VMEM((n_half, m, n2), BF16),
                        pltpu.VMEM((n_half, nb, n2), F32)],
        compiler_params=pltpu.CompilerParams(
            dimension_semantics=("arbitrary",), vmem_limit_bytes=_vmem_limit(est)),
        name="s5_scan_bwd" if reverse else "s5_scan_fwd",
    )(u, a, wb, wc, add)


def _ssm_params(a_re, a_im, log_dt, b_re, b_im, c_re, c_im, nb):
    g, p = a_re.shape
    h = b_re.shape[-1]
    n_half = (g * h) // V7X_MXU_DIM
    gh = g // n_half
    dt = jnp.exp(log_dt)[:, None]
    mag = jnp.exp(dt * a_re)
    ab_re = mag * jnp.cos(dt * a_im)
    ab_im = mag * jnp.sin(dt * a_im)
    den = a_re * a_re + a_im * a_im
    nr, ni = ab_re - 1.0, ab_im
    k_re = (nr * a_re + ni * a_im) / den
    k_im = (ni * a_re - nr * a_im) / den
    bb_re = k_re[..., None] * b_re - k_im[..., None] * b_im
    bb_im = k_re[..., None] * b_im + k_im[..., None] * b_re
    grp_of = lambda n_rows_per_group: jnp.arange(gh * n_rows_per_group) // n_rows_per_group
    in_mask = grp_of(h)[:, None] == grp_of(p)[None, :]
    out_mask = in_mask.T

    def in_proj(bb):
        rows = jnp.swapaxes(bb, 1, 2).reshape(n_half, gh * h, p)
        return jnp.where(in_mask, jnp.tile(rows, (1, 1, gh)), 0.0)

    def out_proj(c):
        rows = jnp.swapaxes(c, 1, 2).reshape(n_half, gh * p, h)
        return jnp.where(out_mask, jnp.tile(rows, (1, 1, gh)), 0.0)

    wb = jnp.concatenate([in_proj(bb_re), in_proj(bb_im)], axis=-1).astype(BF16)
    wc = jnp.concatenate([out_proj(c_re), out_proj(-c_im)], axis=1).astype(BF16)
    a = jnp.concatenate([ab_re.reshape(n_half, gh * p), ab_im.reshape(n_half, gh * p)], axis=-1)
    a = jnp.broadcast_to(a[:, None, :], (n_half, nb, 2 * gh * p))
    return a, wb, wc


def _mix_out_kernel(x_ref, at_ref, y_ref, wglu_ref, bglu_ref, gs_ref, wo_ref, o_ref, z_scr, gate_scr):
    d_attn = at_ref.shape[-1]
    n_sub, sub, _ = z_scr.shape
    rows = [slice(r * sub, (r + 1) * sub) for r in range(n_sub)]

    def stage_z(r):
        z_scr[r] = jax.nn.gelu(y_ref[0, rows[r], :])

    def stage_gate(r):
        gate_scr[r] = jnp.dot(z_scr[r].astype(BF16), wglu_ref[...], preferred_element_type=F32)
        o_ref[0, rows[r], :] = x_ref[0, rows[r], :] + jnp.dot(at_ref[0, rows[r], :], wo_ref[:d_attn, :],
                                                             preferred_element_type=F32)

    def stage_out(r):
        s = z_scr[r] * jax.nn.sigmoid(gate_scr[r] + bglu_ref[...])
        s_n = _rms(s, gs_ref[...]).astype(BF16)
        o_ref[0, rows[r], :] += jnp.dot(s_n, wo_ref[d_attn:, :], preferred_element_type=F32)

    stage_z(0)
    for r in range(n_sub):
        stage_gate(r)
        if r + 1 < n_sub:
            stage_z(r + 1)
        if r > 0:
            stage_out(r - 1)
    stage_out(n_sub - 1)


def _mix_out(x3, attn_n, y, params, layer, *, tm=512, n_sub=2):
    b, l, dm = x3.shape
    d_attn = attn_n.shape[-1]
    d_ssm = y.shape[-1]
    xrow = pl.BlockSpec((1, tm, dm), lambda bi, i: (bi, i, 0))
    srow = pl.BlockSpec((1, tm, d_ssm), lambda bi, i: (bi, i, 0))
    est = 2 * (params[0][0].size + params[3][0].size) + 4 * tm * dm * 4 + 8 * tm * d_ssm * 4
    return pl.pallas_call(
        _mix_out_kernel,
        out_shape=jax.ShapeDtypeStruct((b, l, dm), F32),
        grid=(b, l // tm),
        in_specs=[xrow,
                  pl.BlockSpec((1, tm, d_attn), lambda bi, i: (bi, i, 0)),
                  srow] + [_layer_spec(p, layer) for p in params],
        out_specs=xrow,
        scratch_shapes=[pltpu.VMEM((n_sub, tm // n_sub, d_ssm), F32),
                        pltpu.VMEM((n_sub, tm // n_sub, d_ssm), F32)],
        compiler_params=pltpu.CompilerParams(
            dimension_semantics=("arbitrary", "arbitrary"), vmem_limit_bytes=_vmem_limit(est)),
        name="mix_out",
    )(x3, attn_n, y, *params)


def _rows(v):
    return v.astype(F32)[:, None, :]


def _ffn_params(norm, w_gate, w_up, w_down):
    return _rows(norm), w_gate.astype(BF16), w_up.astype(BF16), (0.5 * w_down).astype(BF16)


def kernel(x, rel_bias_table, ffn1_norm, ffn1_w_gate, ffn1_w_up, ffn1_w_down, mix_norm, w_in, attn_sink, ssm_a_re, ssm_a_im, ssm_log_dt, ssm_b_re, ssm_b_im, ssm_c_re, ssm_c_im, ssm_d, ssm_w_glu, ssm_b_glu, attn_out_norm, ssm_out_norm, w_out, ffn2_norm, ffn2_w_gate, ffn2_w_up, ffn2_w_down, final_norm):
    b, l, dm = x.shape
    depth = w_in.shape[0]
    d_attn = N_HEADS * HEAD_DIM
    d_kv2 = 2 * N_KV_HEADS * HEAD_DIM
    assert b == V7X_SUBLANES, "the scan keeps one batch sequence per sublane"

    bias_g = _group_bias(_rel_bias(rel_bias_table.astype(F32)))
    ffn1 = _ffn_params(ffn1_norm, ffn1_w_gate, ffn1_w_up, ffn1_w_down)
    ffn2 = _ffn_params(ffn2_norm, ffn2_w_gate, ffn2_w_up, ffn2_w_down)
    mix_g, w_in_b = _rows(mix_norm), w_in.astype(BF16)
    sink, attn_g = attn_sink.astype(F32), _rows(attn_out_norm)
    ssm = jax.vmap(jax.vmap(functools.partial(_ssm_params, nb=b)))(
        ssm_a_re, ssm_a_im, ssm_log_dt, ssm_b_re, ssm_b_im, ssm_c_re, ssm_c_im)
    skip_d = _rows(ssm_d)
    out_p = (ssm_w_glu.astype(BF16), _rows(ssm_b_glu), _rows(ssm_out_norm), w_out.astype(BF16))
    final_g = final_norm.astype(F32).reshape(1, -1)

    for i in range(depth):
        x = _ffn(x.reshape(b * l, dm), ffn1, i).reshape(b, l, dm)
        q, kv, u = _proj(x, mix_g, w_in_b, i, d_attn=d_attn, d_kv2=d_kv2)
        attn_n = _attention(q, kv, bias_g, sink, attn_g, i)
        y = _ssm_scan(u, skip_d, *ssm, i, reverse=False)
        y = _ssm_scan(u, y, *ssm, i, reverse=True)
        x = _mix_out(x, attn_n, y, out_p, i)
        x = _ffn(x.reshape(b * l, dm), ffn2, i, final_g if i == depth - 1 else None).reshape(b, l, dm)
    return x
```

```python
import functools
import math

import jax
import jax.numpy as jnp
from jax import lax
from jax.experimental import pallas as pl
from jax.experimental.pallas import tpu as pltpu

EPS = 1e-6
NEG_INF = -1e30

N_HEADS = 8
N_KV_HEADS = 2
HEAD_DIM = 64
SSM_GROUP = 16
SSM_STATE = 64
ATT_BLOCK = 128
N_BUCKETS = 32
MAX_DISTANCE = 128

V7X_SUBLANES = 8
V7X_LANES = 128
V7X_MXU_DIM = 256
V7X_VMEM_BYTES = 64 * 1024 * 1024

F32 = jnp.float32
BF16 = jnp.bfloat16


def _vmem_limit(nbytes):
    return int(min(nbytes + (8 << 20), V7X_VMEM_BYTES - (6 << 20)))


def _rms(x, g):
    return x * lax.rsqrt(jnp.mean(x * x, axis=-1, keepdims=True) + EPS) * g


def _const_spec(shape):
    nd = len(shape)
    return pl.BlockSpec(shape, lambda *_: (0,) * nd, pipeline_mode=pl.Buffered(1))


def _layer_spec(arr, *idx):
    rest = arr.shape[len(idx):]
    return pl.BlockSpec((None,) * len(idx) + rest, lambda *_: idx + (0,) * len(rest),
                        pipeline_mode=pl.Buffered(1))


def _mixer_out(x, at_ref, y_ref, wglu_ref, bglu_ref, gs_ref, wo_ref):
    d_attn = at_ref.shape[-1]
    z = jax.nn.gelu(y_ref[...])
    gate = jnp.dot(z.astype(BF16), wglu_ref[...], preferred_element_type=F32) + bglu_ref[...]
    s_n = _rms(z * jax.nn.sigmoid(gate), gs_ref[...]).astype(BF16)
    return (x + jnp.dot(at_ref[...], wo_ref[:d_attn, :], preferred_element_type=F32)
            + jnp.dot(s_n, wo_ref[d_attn:, :], preferred_element_type=F32))


def _ffn_kernel(x_ref, *refs, fc, final, mixer):
    if mixer:
        mix_refs, refs = refs[:6], refs[6:]
    g_ref, wg_ref, wu_ref, wd_ref = refs[:4]
    if final:
        fg_ref, o_ref, h_scr, gu_scr = refs[4:]
    else:
        o_ref, h_scr, gu_scr = refs[4:]
    n_chunks = wd_ref.shape[0] // fc
    x = x_ref[...]
    if mixer:
        x = _mixer_out(x, *mix_refs)
    h_scr[...] = _rms(x, g_ref[...]).astype(BF16)
    o_ref[...] = x

    def gate_up(c):
        cols = slice(c * fc, (c + 1) * fc)
        gu_scr[c % 2, :, :fc] = jnp.dot(h_scr[...], wg_ref[:, cols], preferred_element_type=F32)
        gu_scr[c % 2, :, fc:] = jnp.dot(h_scr[...], wu_ref[:, cols], preferred_element_type=F32)

    def down(c):
        gu = gu_scr[c % 2]
        gate, up = gu[:, :fc], gu[:, fc:]
        act = (gate * jax.nn.sigmoid(gate) * up).astype(BF16)
        o_ref[...] += jnp.dot(act, wd_ref[c * fc:(c + 1) * fc, :], preferred_element_type=F32)

    gate_up(0)
    for c in range(n_chunks):
        if c + 1 < n_chunks:
            gate_up(c + 1)
        down(c)
    if final:
        o_ref[...] = _rms(o_ref[...], fg_ref[...])


def _ffn(x2, params, layer, final_g=None, mixer=None, *, tm=512, fc=V7X_MXU_DIM):
    t, d = x2.shape
    final = final_g is not None
    row = pl.BlockSpec((tm, d), lambda i: (i, 0))
    in_specs, args = [row], [x2]
    est = 2 * sum(p[0].size for p in params[1:]) + 4 * tm * d * 4 + tm * d * 2 + 4 * tm * fc * 4
    if mixer is not None:
        attn_n, y, mix_params = mixer
        in_specs += [pl.BlockSpec((tm, attn_n.shape[-1]), lambda i: (i, 0)),
                     pl.BlockSpec((tm, y.shape[-1]), lambda i: (i, 0))]
        in_specs += [_layer_spec(p, layer) for p in mix_params]
        args += [attn_n, y, *mix_params]
        est += 2 * (mix_params[0][0].size + mix_params[3][0].size) + 12 * tm * y.shape[-1] * 4
    in_specs += [_layer_spec(p, layer) for p in params]
    args += list(params)
    if final:
        in_specs.append(_const_spec((1, d)))
        args.append(final_g)
    return pl.pallas_call(
        functools.partial(_ffn_kernel, fc=fc, final=final, mixer=mixer is not None),
        out_shape=jax.ShapeDtypeStruct((t, d), F32),
        grid=(t // tm,),
        in_specs=in_specs,
        out_specs=row,
        scratch_shapes=[pltpu.VMEM((tm, d), BF16), pltpu.VMEM((2, tm, 2 * fc), F32)],
        compiler_params=pltpu.CompilerParams(
            dimension_semantics=("arbitrary",), vmem_limit_bytes=_vmem_limit(est)),
        name="mix_ffn" if mixer is not None else "ffn",
    )(*args)


def _proj_kernel(x_ref, g_ref, w_ref, q_ref, kv_ref, u_ref, *, d_attn, d_kv2):
    h = _rms(x_ref[0], g_ref[...]).astype(BF16)
    p = jnp.dot(h, w_ref[...], preferred_element_type=F32)
    q_ref[0] = (p[:, :d_attn] * (HEAD_DIM ** -0.5)).astype(BF16)
    kv_ref[0] = p[:, d_attn:d_attn + d_kv2].astype(BF16)
    u_ref[0] = p[:, d_attn + d_kv2:]


def _proj(x3, g, w_in, layer, *, d_attn, d_kv2, tm=512):
    b, l, d = x3.shape
    d_in = w_in.shape[-1]
    d_ssm = d_in - d_attn - d_kv2
    est = 2 * w_in[0].size + 2 * tm * d * 4 + 3 * tm * d_in * 4
    return pl.pallas_call(
        functools.partial(_proj_kernel, d_attn=d_attn, d_kv2=d_kv2),
        out_shape=(
            jax.ShapeDtypeStruct((b, l, d_attn), BF16),
            jax.ShapeDtypeStruct((b, l, d_kv2), BF16),
            jax.ShapeDtypeStruct((b, l, d_ssm), F32),
        ),
        grid=(b, l // tm),
        in_specs=[pl.BlockSpec((1, tm, d), lambda bi, i: (bi, i, 0)),
                  _layer_spec(g, layer), _layer_spec(w_in, layer)],
        out_specs=(
            pl.BlockSpec((1, tm, d_attn), lambda bi, i: (bi, i, 0)),
            pl.BlockSpec((1, tm, d_kv2), lambda bi, i: (bi, i, 0)),
            pl.BlockSpec((1, tm, d_ssm), lambda bi, i: (bi, i, 0)),
        ),
        compiler_params=pltpu.CompilerParams(
            dimension_semantics=("arbitrary", "arbitrary"), vmem_limit_bytes=_vmem_limit(est)),
        name="mix_proj",
    )(x3, g, w_in)


def _bias_kernel(table_ref, idx_ref, band_ref, o_ref):
    idx = idx_ref[...]
    band = band_ref[...] > 0
    for h in range(N_HEADS):
        acc = jnp.zeros(idx.shape, F32)
        for bkt in range(N_BUCKETS):
            acc = jnp.where(idx == bkt, table_ref[bkt, h], acc)
        o_ref[h] = jnp.where(band, acc, NEG_INF)


def _t5_bucket(rel):
    half = N_BUCKETS // 2
    max_exact = half // 2
    ret = jnp.where(rel > 0, half, 0)
    n = jnp.abs(rel)
    nf = jnp.maximum(n, 1).astype(F32)
    large = max_exact + (jnp.log(nf / max_exact) / math.log(MAX_DISTANCE / max_exact)
                         * (half - max_exact)).astype(jnp.int32)
    large = jnp.minimum(large, half - 1)
    return ret + jnp.where(n < max_exact, n, large)


def _rel_bias(table):
    rel = (jnp.arange(3 * ATT_BLOCK)[None, :] - ATT_BLOCK) - jnp.arange(ATT_BLOCK)[:, None]
    idx = _t5_bucket(rel).astype(jnp.int32)
    band = (jnp.abs(rel) <= ATT_BLOCK).astype(jnp.int32)
    return pl.pallas_call(
        _bias_kernel,
        out_shape=jax.ShapeDtypeStruct((N_HEADS, ATT_BLOCK, 3 * ATT_BLOCK), F32),
        in_specs=[pl.BlockSpec(memory_space=pltpu.SMEM),
                  pl.BlockSpec(memory_space=pltpu.VMEM),
                  pl.BlockSpec(memory_space=pltpu.VMEM)],
        out_specs=pl.BlockSpec(memory_space=pltpu.VMEM),
        name="rel_bias",
    )(table, idx, band)


def _attn_kernel(sink_ref, q_ref, kvp_ref, kvm_ref, kvn_ref, bias_ref, g_ref, o_ref,
                 kd_scr, ve_scr, vo_scr, s_scr, *, nq, layer):
    i = pl.program_id(1)
    n_i = pl.num_programs(1)
    blk = ATT_BLOCK
    dh = HEAD_DIM
    dk2 = N_KV_HEADS * dh
    grp = N_HEADS // N_KV_HEADS
    assert dk2 == V7X_LANES and grp == 4
    lane = lax.broadcasted_iota(jnp.int32, (1, dk2), 1)
    m_lo = (lane < dh).astype(BF16)
    m_hi = (lane >= dh).astype(BF16)

    kv = jnp.concatenate([kvp_ref[0], kvm_ref[0], kvn_ref[0]], axis=0)
    rows = kv.shape[0]
    k_row, v_row = kv[:, :dk2], kv[:, dk2:]
    k_swap = jnp.concatenate([k_row[:, dh:], k_row[:, :dh]], axis=-1)
    v_swap = jnp.concatenate([v_row[:, dh:], v_row[:, :dh]], axis=-1)
    ones_lo = jnp.broadcast_to(m_lo, (rows, dk2))
    ones_hi = jnp.broadcast_to(m_hi, (rows, dk2))
    kd_scr[0] = k_row * m_lo + k_swap * m_hi
    kd_scr[1] = k_swap * m_lo + k_row * m_hi
    ve_scr[0] = jnp.concatenate([v_row * m_lo, ones_lo], axis=-1)
    vo_scr[0] = jnp.concatenate([v_swap * m_hi, ones_hi], axis=-1)
    ve_scr[1] = jnp.concatenate([v_swap * m_lo, ones_lo], axis=-1)
    vo_scr[1] = jnp.concatenate([v_row * m_hi, ones_hi], axis=-1)

    col = lax.broadcasted_iota(jnp.int32, (1, 3 * blk), 1)
    edge_first = jnp.where(col < blk, jnp.where(i == 0, NEG_INF, 0.0), 0.0)
    edge_last = jnp.where(col >= 2 * blk, jnp.where(i == n_i - 1, NEG_INF, 0.0), 0.0)
    lane_lo = lane < dh

    def scores(jb, kh):
        qs = []
        for msk in (m_lo, m_hi):
            for pr in range(grp // 2):
                c0 = kh * grp * dh + pr * dk2
                qs.append(q_ref[0, jb * blk:(jb + 1) * blk, c0:c0 + dk2] * msk)
        q_st = jnp.concatenate(qs, axis=0)
        s = lax.dot_general(q_st, kd_scr[kh, jb * blk:(jb + 3) * blk, :], (((1,), (1,)), ((), ())),
                            preferred_element_type=F32)
        s = s + bias_ref[kh]
        if jb == 0:
            s = s + edge_first
        if jb == nq - 1:
            s = s + edge_last
        return s

    tiles = [(jb, kh) for jb in range(nq) for kh in range(N_KV_HEADS)]
    s_scr[0] = scores(*tiles[0])
    pieces = []
    for n, (jb, kh) in enumerate(tiles):
        win = slice(jb * blk, (jb + 3) * blk)
        if n + 1 < len(tiles):
            s_scr[(n + 1) % 2] = scores(*tiles[n + 1])
        s = s_scr[n % 2]
        s_max = jnp.maximum(jnp.maximum(s[:, :blk], s[:, blk:2 * blk]), s[:, 2 * blk:])
        p, t = [], []
        for r, hh in enumerate((0, 2, 1, 3)):
            rs = slice(r * blk, (r + 1) * blk)
            sink = sink_ref[layer, kh * grp + hh]
            m = jnp.maximum(jnp.max(s_max[rs], axis=-1, keepdims=True), sink)
            p.append(jnp.exp(s[rs] - m).astype(BF16))
            t.append(jnp.exp(sink - m))
        for pr in range(grp // 2):
            ev, od = pr, grp // 2 + pr
            nd = (jnp.dot(p[ev], ve_scr[kh, win, :], preferred_element_type=F32)
                  + jnp.dot(p[od], vo_scr[kh, win, :], preferred_element_type=F32))
            den = nd[:, dk2:] + jnp.where(lane_lo, t[ev], t[od])
            pieces.append(nd[:, :dk2] / den)
        if kh == N_KV_HEADS - 1:
            o = jnp.concatenate(pieces, axis=-1)
            o_ref[0, jb * blk:(jb + 1) * blk, :] = _rms(o, g_ref[...]).astype(BF16)
            pieces = []


def _group_bias(bias):
    grp = N_HEADS // N_KV_HEADS
    bias_g = bias.reshape(N_KV_HEADS, grp, ATT_BLOCK, 3 * ATT_BLOCK)[:, jnp.array([0, 2, 1, 3])]
    return bias_g.reshape(N_KV_HEADS, grp * ATT_BLOCK, 3 * ATT_BLOCK)


def _attention(q, kv, bias_g, sink, g, layer, *, nq=4):
    b, l, d_attn = q.shape
    d_kv2 = kv.shape[-1]
    blk = ATT_BLOCK
    tq = nq * blk
    nb = l // blk
    grp = N_HEADS // N_KV_HEADS
    rows = tq + 2 * blk
    est = (bias_g.size * 4 + 6 * tq * (d_attn + d_kv2) * 2 + N_KV_HEADS * rows * 5 * V7X_LANES * 2
           + 6 * grp * blk * 3 * blk * 4)
    return pl.pallas_call(
        functools.partial(_attn_kernel, nq=nq, layer=layer),
        out_shape=jax.ShapeDtypeStruct((b, l, d_attn), BF16),
        grid=(b, l // tq),
        in_specs=[
            pl.BlockSpec(memory_space=pltpu.SMEM),
            pl.BlockSpec((1, tq, d_attn), lambda bi, i: (bi, i, 0)),
            pl.BlockSpec((1, blk, d_kv2), lambda bi, i: (bi, jnp.maximum(i * nq - 1, 0), 0)),
            pl.BlockSpec((1, tq, d_kv2), lambda bi, i: (bi, i, 0)),
            pl.BlockSpec((1, blk, d_kv2), lambda bi, i: (bi, jnp.minimum((i + 1) * nq, nb - 1), 0)),
            _const_spec(bias_g.shape),
            _layer_spec(g, layer),
        ],
        out_specs=pl.BlockSpec((1, tq, d_attn), lambda bi, i: (bi, i, 0)),
        scratch_shapes=[pltpu.VMEM((N_KV_HEADS, rows, V7X_LANES), BF16),
                        pltpu.VMEM((N_KV_HEADS, rows, 2 * V7X_LANES), BF16),
                        pltpu.VMEM((N_KV_HEADS, rows, 2 * V7X_LANES), BF16),
                        pltpu.VMEM((2, grp * blk, 3 * blk), F32)],
        compiler_params=pltpu.CompilerParams(
            dimension_semantics=("arbitrary", "arbitrary"), vmem_limit_bytes=_vmem_limit(est)),
        name="band_attn",
    )(sink, q, kv, kv, kv, bias_g, g)


def _ssm_kernel(u_ref, a_ref, wb_ref, wc_ref, add_ref, y_ref, ut_scr, yt_scr, x_scr, s_scr, st_scr,
                *, tc, nb, reverse, lane_tile):
    n_half, _, n2 = a_ref.shape
    nre = n2 // 2
    kh = wb_ref.shape[1]
    rows2 = 2 * nb
    n_pairs = tc // 2
    n_slab = ut_scr.shape[0]
    slab_per_half = kh // V7X_LANES

    @pl.when(pl.program_id(0) == 0)
    def _():
        st_scr[...] = jnp.zeros_like(st_scr)

    for bi in range(nb):
        for sl in range(n_slab):
            ut_scr.at[sl][pl.ds(bi, tc, stride=nb), :] = u_ref[bi, :, sl * V7X_LANES:(sl + 1) * V7X_LANES]

    for hf in range(n_half):
        u_h = jnp.concatenate([ut_scr[hf * slab_per_half + k] for k in range(slab_per_half)], axis=-1)
        x_scr[hf] = jnp.dot(u_h.astype(BF16), wb_ref[hf], preferred_element_type=F32)
    for hf in range(n_half):
        for lt in range(nre // lane_tile):
            re_sl = slice(lt * lane_tile, (lt + 1) * lane_tile)
            im_sl = slice(nre + lt * lane_tile, nre + (lt + 1) * lane_tile)
            a_re = a_ref[hf, :, re_sl]
            a_im = a_ref[hf, :, im_sl]

            def step(s_re, s_im, x_re, x_im):
                n_re = a_re * s_re - a_im * s_im + x_re
                n_im = a_re * s_im + a_im * s_re + x_im
                return n_re, n_im

            s_re, s_im = st_scr[hf, :, re_sl], st_scr[hf, :, im_sl]
            for j in range(n_pairs):
                r0 = ((n_pairs - 1 - j) if reverse else j) * rows2
                x_re = x_scr[hf, r0:r0 + rows2, re_sl]
                x_im = x_scr[hf, r0:r0 + rows2, im_sl]
                first, second = (slice(nb, rows2), slice(0, nb)) if reverse else (slice(0, nb), slice(nb, rows2))
                s1_re, s1_im = step(s_re, s_im, x_re[first], x_im[first])
                s_re, s_im = step(s1_re, s1_im, x_re[second], x_im[second])
                lo_re, hi_re = (s_re, s1_re) if reverse else (s1_re, s_re)
                lo_im, hi_im = (s_im, s1_im) if reverse else (s1_im, s_im)
                s_scr[hf, r0:r0 + rows2, re_sl] = jnp.concatenate([lo_re, hi_re], axis=0).astype(BF16)
                s_scr[hf, r0:r0 + rows2, im_sl] = jnp.concatenate([lo_im, hi_im], axis=0).astype(BF16)
            st_scr[hf, :, re_sl] = s_re
            st_scr[hf, :, im_sl] = s_im
        mh = (tc * nb) // 2
        for r in range(2):
            y_h = jnp.dot(s_scr[hf, r * mh:(r + 1) * mh, :], wc_ref[hf], preferred_element_type=F32)
            for k in range(slab_per_half):
                yt_scr[hf * slab_per_half + k, r * mh:(r + 1) * mh, :] = y_h[:, k * V7X_LANES:(k + 1) * V7X_LANES]

    for bi in range(nb):
        for sl in range(n_slab):
            lanes = slice(sl * V7X_LANES, (sl + 1) * V7X_LANES)
            y = yt_scr.at[sl][pl.ds(bi, tc, stride=nb), :]
            if reverse:
                y = y + add_ref[bi, :, lanes]
            else:
                y = y + add_ref[:, lanes] * u_ref[bi, :, lanes]
            y_ref[bi, :, lanes] = y


def _ssm_scan(u, add, a, wb, wc, layer, *, reverse, tc=64, lane_tile=512):
    nb, l, d_ssm = u.shape
    n_half, kh, n2 = wb.shape[2:]
    m = tc * nb
    n_t = l // tc
    di = int(reverse)
    idx = (lambda i: (0, n_t - 1 - i, 0)) if reverse else (lambda i: (0, i, 0))
    est = (6 * m * d_ssm * 4 + n_half * m * n2 * 6
           + 2 * (wb[0, 0].size + wc[0, 0].size) + 4 * a[0, 0].size)
    return pl.pallas_call(
        functools.partial(_ssm_kernel, tc=tc, nb=nb, reverse=reverse, lane_tile=lane_tile),
        out_shape=jax.ShapeDtypeStruct((nb, l, d_ssm), F32),
        grid=(n_t,),
        in_specs=[pl.BlockSpec((nb, tc, d_ssm), idx),
                  _layer_spec(a, layer, di), _layer_spec(wb, layer, di), _layer_spec(wc, layer, di),
                  pl.BlockSpec((nb, tc, d_ssm), idx) if reverse else _layer_spec(add, layer)],
        out_specs=pl.BlockSpec((nb, tc, d_ssm), idx),
        scratch_shapes=[pltpu.VMEM((d_ssm // V7X_LANES, m, V7X_LANES), F32),
                        pltpu.VMEM((d_ssm // V7X_LANES, m, V7X_LANES), F32),
                        pltpu.VMEM((n_half, m, n2), F32),
                        pltpu.VMEM((n_half, m, n2), BF16),
                        pltpu.VMEM((n_half, nb, n2), F32)],
        compiler_params=pltpu.CompilerParams(
            dimension_semantics=("arbitrary",), vmem_limit_bytes=_vmem_limit(est)),
        name="s5_scan_bwd" if reverse else "s5_scan_fwd",
    )(u, a, wb, wc, add)


def _ssm_params(a_re, a_im, log_dt, b_re, b_im, c_re, c_im, nb):
    g, p = a_re.shape
    h = b_re.shape[-1]
    n_half = (g * h) // V7X_MXU_DIM
    gh = g // n_half
    dt = jnp.exp(log_dt)[:, None]
    mag = jnp.exp(dt * a_re)
    ab_re = mag * jnp.cos(dt * a_im)
    ab_im = mag * jnp.sin(dt * a_im)
    den = a_re * a_re + a_im * a_im
    nr, ni = ab_re - 1.0, ab_im
    k_re = (nr * a_re + ni * a_im) / den
    k_im = (ni * a_re - nr * a_im) / den
    bb_re = k_re[..., None] * b_re - k_im[..., None] * b_im
    bb_im = k_re[..., None] * b_im + k_im[..., None] * b_re
    grp_of = lambda n_rows_per_group: jnp.arange(gh * n_rows_per_group) // n_rows_per_group
    in_mask = grp_of(h)[:, None] == grp_of(p)[None, :]
    out_mask = in_mask.T

    def in_proj(bb):
        rows = jnp.swapaxes(bb, 1, 2).reshape(n_half, gh * h, p)
        return jnp.where(in_mask, jnp.tile(rows, (1, 1, gh)), 0.0)

    def out_proj(c):
        rows = jnp.swapaxes(c, 1, 2).reshape(n_half, gh * p, h)
        return jnp.where(out_mask, jnp.tile(rows, (1, 1, gh)), 0.0)

    wb = jnp.concatenate([in_proj(bb_re), in_proj(bb_im)], axis=-1).astype(BF16)
    wc = jnp.concatenate([out_proj(c_re), out_proj(-c_im)], axis=1).astype(BF16)
    a = jnp.concatenate([ab_re.reshape(n_half, gh * p), ab_im.reshape(n_half, gh * p)], axis=-1)
    a = jnp.broadcast_to(a[:, None, :], (n_half, nb, 2 * gh * p))
    return a, wb, wc


def _rows(v):
    return v.astype(F32)[:, None, :]


def _ffn_params(norm, w_gate, w_up, w_down):
    return _rows(norm), w_gate.astype(BF16), w_up.astype(BF16), (0.5 * w_down).astype(BF16)


def kernel(x, rel_bias_table, ffn1_norm, ffn1_w_gate, ffn1_w_up, ffn1_w_down, mix_norm, w_in, attn_sink, ssm_a_re, ssm_a_im, ssm_log_dt, ssm_b_re, ssm_b_im, ssm_c_re, ssm_c_im, ssm_d, ssm_w_glu, ssm_b_glu, attn_out_norm, ssm_out_norm, w_out, ffn2_norm, ffn2_w_gate, ffn2_w_up, ffn2_w_down, final_norm):
    b, l, dm = x.shape
    depth = w_in.shape[0]
    d_attn = N_HEADS * HEAD_DIM
    d_kv2 = 2 * N_KV_HEADS * HEAD_DIM
    assert b == V7X_SUBLANES, "the scan keeps one batch sequence per sublane"

    bias_g = _group_bias(_rel_bias(rel_bias_table.astype(F32)))
    ffn1 = _ffn_params(ffn1_norm, ffn1_w_gate, ffn1_w_up, ffn1_w_down)
    ffn2 = _ffn_params(ffn2_norm, ffn2_w_gate, ffn2_w_up, ffn2_w_down)
    mix_g, w_in_b = _rows(mix_norm), w_in.astype(BF16)
    sink, attn_g = attn_sink.astype(F32), _rows(attn_out_norm)
    ssm = jax.vmap(jax.vmap(functools.partial(_ssm_params, nb=b)))(
        ssm_a_re, ssm_a_im, ssm_log_dt, ssm_b_re, ssm_b_im, ssm_c_re, ssm_c_im)
    skip_d = _rows(ssm_d)
    out_p = (ssm_w_glu.astype(BF16), _rows(ssm_b_glu), _rows(ssm_out_norm), w_out.astype(BF16))
    final_g = final_norm.astype(F32).reshape(1, -1)

    for i in range(depth):
        x = _ffn(x.reshape(b * l, dm), ffn1, i).reshape(b, l, dm)
        q, kv, u = _proj(x, mix_g, w_in_b, i, d_attn=d_attn, d_kv2=d_kv2)
        attn_n = _attention(q, kv, bias_g, sink, attn_g, i)
        y = _ssm_scan(u, skip_d, *ssm, i, reverse=False)
        y = _ssm_scan(u, y, *ssm, i, reverse=True)
        mixer = (attn_n.reshape(b * l, -1), y.reshape(b * l, -1), out_p)
        x = _ffn(x.reshape(b * l, dm), ffn2, i, final_g if i == depth - 1 else None, mixer).reshape(b, l, dm)
    return x
```

```python
import functools
import math

import jax
import jax.numpy as jnp
from jax import lax
from jax.experimental import pallas as pl
from jax.experimental.pallas import tpu as pltpu

EPS = 1e-6
NEG_INF = -1e30

N_HEADS = 8
N_KV_HEADS = 2
HEAD_DIM = 64
SSM_GROUP = 16
SSM_STATE = 64
ATT_BLOCK = 128
N_BUCKETS = 32
MAX_DISTANCE = 128

V7X_SUBLANES = 8
V7X_LANES = 128
V7X_MXU_DIM = 256
V7X_VMEM_BYTES = 64 * 1024 * 1024

F32 = jnp.float32
BF16 = jnp.bfloat16


def _vmem_limit(nbytes):
    return int(min(nbytes + (8 << 20), V7X_VMEM_BYTES - (6 << 20)))


def _rms(x, g):
    return x * lax.rsqrt(jnp.mean(x * x, axis=-1, keepdims=True) + EPS) * g


def _const_spec(shape):
    nd = len(shape)
    return pl.BlockSpec(shape, lambda *_: (0,) * nd, pipeline_mode=pl.Buffered(1))


def _layer_spec(arr, *idx):
    rest = arr.shape[len(idx):]
    return pl.BlockSpec((None,) * len(idx) + rest, lambda *_: idx + (0,) * len(rest),
                        pipeline_mode=pl.Buffered(1))


def _mixer_out(x, at_ref, y_ref, wglu_ref, bglu_ref, gs_ref, wo_ref):
    d_attn = at_ref.shape[-1]
    z = jax.nn.gelu(y_ref[...])
    gate = jnp.dot(z.astype(BF16), wglu_ref[...], preferred_element_type=F32) + bglu_ref[...]
    s_n = _rms(z * jax.nn.sigmoid(gate), gs_ref[...]).astype(BF16)
    return (x + jnp.dot(at_ref[...], wo_ref[:d_attn, :], preferred_element_type=F32)
            + jnp.dot(s_n, wo_ref[d_attn:, :], preferred_element_type=F32))


def _ffn_kernel(x_ref, *refs, fc, final, mixer):
    if mixer:
        mix_refs, refs = refs[:6], refs[6:]
    g_ref, wg_ref, wu_ref, wd_ref = refs[:4]
    if final:
        fg_ref, o_ref, h_scr, gu_scr = refs[4:]
    else:
        o_ref, h_scr, gu_scr = refs[4:]
    n_chunks = wd_ref.shape[0] // fc
    x = x_ref[...]
    if mixer:
        x = _mixer_out(x, *mix_refs)
    h_scr[...] = _rms(x, g_ref[...]).astype(BF16)
    o_ref[...] = x

    def gate_up(c):
        cols = slice(c * fc, (c + 1) * fc)
        gu_scr[c % 2, :, :fc] = jnp.dot(h_scr[...], wg_ref[:, cols], preferred_element_type=F32)
        gu_scr[c % 2, :, fc:] = jnp.dot(h_scr[...], wu_ref[:, cols], preferred_element_type=F32)

    def down(c):
        gu = gu_scr[c % 2]
        gate, up = gu[:, :fc], gu[:, fc:]
        act = (gate * jax.nn.sigmoid(gate) * up).astype(BF16)
        o_ref[...] += jnp.dot(act, wd_ref[c * fc:(c + 1) * fc, :], preferred_element_type=F32)

    gate_up(0)
    for c in range(n_chunks):
        if c + 1 < n_chunks:
            gate_up(c + 1)
        down(c)
    if final:
        o_ref[...] = _rms(o_ref[...], fg_ref[...])


def _ffn(x2, params, layer, final_g=None, mixer=None, *, tm=512, fc=V7X_MXU_DIM):
    t, d = x2.shape
    final = final_g is not None
    row = pl.BlockSpec((tm, d), lambda i: (i, 0))
    in_specs, args = [row], [x2]
    est = 2 * sum(p[0].size for p in params[1:]) + 4 * tm * d * 4 + tm * d * 2 + 4 * tm * fc * 4
    if mixer is not None:
        attn_n, y, mix_params = mixer
        in_specs += [pl.BlockSpec((tm, attn_n.shape[-1]), lambda i: (i, 0)),
                     pl.BlockSpec((tm, y.shape[-1]), lambda i: (i, 0))]
        in_specs += [_layer_spec(p, layer) for p in mix_params]
        args += [attn_n, y, *mix_params]
        est += 2 * (mix_params[0][0].size + mix_params[3][0].size) + 12 * tm * y.shape[-1] * 4
    in_specs += [_layer_spec(p, layer) for p in params]
    args += list(params)
    if final:
        in_specs.append(_const_spec((1, d)))
        args.append(final_g)
    return pl.pallas_call(
        functools.partial(_ffn_kernel, fc=fc, final=final, mixer=mixer is not None),
        out_shape=jax.ShapeDtypeStruct((t, d), F32),
        grid=(t // tm,),
        in_specs=in_specs,
        out_specs=row,
        scratch_shapes=[pltpu.VMEM((tm, d), BF16), pltpu.VMEM((2, tm, 2 * fc), F32)],
        compiler_params=pltpu.CompilerParams(
            dimension_semantics=("arbitrary",), vmem_limit_bytes=_vmem_limit(est)),
        name="mix_ffn" if mixer is not None else "ffn",
    )(*args)


def _proj_kernel(x_ref, g_ref, w_ref, q_ref, kv_ref, u_ref, *, d_attn, d_kv2):
    h = _rms(x_ref[0], g_ref[...]).astype(BF16)
    p = jnp.dot(h, w_ref[...], preferred_element_type=F32)
    q_ref[0] = (p[:, :d_attn] * (HEAD_DIM ** -0.5)).astype(BF16)
    kv_ref[0] = p[:, d_attn:d_attn + d_kv2].astype(BF16)
    u_ref[0] = p[:, d_attn + d_kv2:]


def _proj(x3, g, w_in, layer, *, d_attn, d_kv2, tm=512):
    b, l, d = x3.shape
    d_in = w_in.shape[-1]
    d_ssm = d_in - d_attn - d_kv2
    est = 2 * w_in[0].size + 2 * tm * d * 4 + 3 * tm * d_in * 4
    return pl.pallas_call(
        functools.partial(_proj_kernel, d_attn=d_attn, d_kv2=d_kv2),
        out_shape=(
            jax.ShapeDtypeStruct((b, l, d_attn), BF16),
            jax.ShapeDtypeStruct((b, l, d_kv2), BF16),
            jax.ShapeDtypeStruct((b, l, d_ssm), F32),
        ),
        grid=(b, l // tm),
        in_specs=[pl.BlockSpec((1, tm, d), lambda bi, i: (bi, i, 0)),
                  _layer_spec(g, layer), _layer_spec(w_in, layer)],
        out_specs=(
            pl.BlockSpec((1, tm, d_attn), lambda bi, i: (bi, i, 0)),
            pl.BlockSpec((1, tm, d_kv2), lambda bi, i: (bi, i, 0)),
            pl.BlockSpec((1, tm, d_ssm), lambda bi, i: (bi, i, 0)),
        ),
        compiler_params=pltpu.CompilerParams(
            dimension_semantics=("arbitrary", "arbitrary"), vmem_limit_bytes=_vmem_limit(est)),
        name="mix_proj",
    )(x3, g, w_in)


def _bias_kernel(table_ref, idx_ref, band_ref, o_ref):
    idx = idx_ref[...]
    band = band_ref[...] > 0
    for h in range(N_HEADS):
        acc = jnp.zeros(idx.shape, F32)
        for bkt in range(N_BUCKETS):
            acc = jnp.where(idx == bkt, table_ref[bkt, h], acc)
        o_ref[h] = jnp.where(band, acc, NEG_INF)


def _t5_bucket(rel):
    half = N_BUCKETS // 2
    max_exact = half // 2
    ret = jnp.where(rel > 0, half, 0)
    n = jnp.abs(rel)
    nf = jnp.maximum(n, 1).astype(F32)
    large = max_exact + (jnp.log(nf / max_exact) / math.log(MAX_DISTANCE / max_exact)
                         * (half - max_exact)).astype(jnp.int32)
    large = jnp.minimum(large, half - 1)
    return ret + jnp.where(n < max_exact, n, large)


def _rel_bias(table):
    rel = (jnp.arange(3 * ATT_BLOCK)[None, :] - ATT_BLOCK) - jnp.arange(ATT_BLOCK)[:, None]
    idx = _t5_bucket(rel).astype(jnp.int32)
    band = (jnp.abs(rel) <= ATT_BLOCK).astype(jnp.int32)
    return pl.pallas_call(
        _bias_kernel,
        out_shape=jax.ShapeDtypeStruct((N_HEADS, ATT_BLOCK, 3 * ATT_BLOCK), F32),
        in_specs=[pl.BlockSpec(memory_space=pltpu.SMEM),
                  pl.BlockSpec(memory_space=pltpu.VMEM),
                  pl.BlockSpec(memory_space=pltpu.VMEM)],
        out_specs=pl.BlockSpec(memory_space=pltpu.VMEM),
        name="rel_bias",
    )(table, idx, band)


def _attn_kernel(sink_ref, q_ref, kvp_ref, kvm_ref, kvn_ref, bias_ref, g_ref, o_ref,
                 kd_scr, ve_scr, vo_scr, s_scr, *, nq, layer):
    i = pl.program_id(1)
    n_i = pl.num_programs(1)
    blk = ATT_BLOCK
    dh = HEAD_DIM
    dk2 = N_KV_HEADS * dh
    grp = N_HEADS // N_KV_HEADS
    assert dk2 == V7X_LANES and grp == 4
    lane = lax.broadcasted_iota(jnp.int32, (1, dk2), 1)
    m_lo = (lane < dh).astype(BF16)
    m_hi = (lane >= dh).astype(BF16)

    kv = jnp.concatenate([kvp_ref[0], kvm_ref[0], kvn_ref[0]], axis=0)
    rows = kv.shape[0]
    k_row, v_row = kv[:, :dk2], kv[:, dk2:]
    k_swap = jnp.concatenate([k_row[:, dh:], k_row[:, :dh]], axis=-1)
    v_swap = jnp.concatenate([v_row[:, dh:], v_row[:, :dh]], axis=-1)
    ones_lo = jnp.broadcast_to(m_lo, (rows, dk2))
    ones_hi = jnp.broadcast_to(m_hi, (rows, dk2))
    kd_scr[0] = k_row * m_lo + k_swap * m_hi
    kd_scr[1] = k_swap * m_lo + k_row * m_hi
    ve_scr[0] = jnp.concatenate([v_row * m_lo, ones_lo], axis=-1)
    vo_scr[0] = jnp.concatenate([v_swap * m_hi, ones_hi], axis=-1)
    ve_scr[1] = jnp.concatenate([v_swap * m_lo, ones_lo], axis=-1)
    vo_scr[1] = jnp.concatenate([v_row * m_hi, ones_hi], axis=-1)

    col = lax.broadcasted_iota(jnp.int32, (1, 3 * blk), 1)
    edge_first = jnp.where(col < blk, jnp.where(i == 0, NEG_INF, 0.0), 0.0)
    edge_last = jnp.where(col >= 2 * blk, jnp.where(i == n_i - 1, NEG_INF, 0.0), 0.0)
    lane_lo = lane < dh

    def scores(jb, kh):
        qs = []
        for msk in (m_lo, m_hi):
            for pr in range(grp // 2):
                c0 = kh * grp * dh + pr * dk2
                qs.append(q_ref[0, jb * blk:(jb + 1) * blk, c0:c0 + dk2] * msk)
        q_st = jnp.concatenate(qs, axis=0)
        s = lax.dot_general(q_st, kd_scr[kh, jb * blk:(jb + 3) * blk, :], (((1,), (1,)), ((), ())),
                            preferred_element_type=F32)
        s = s + bias_ref[kh]
        if jb == 0:
            s = s + edge_first
        if jb == nq - 1:
            s = s + edge_last
        return s

    tiles = [(jb, kh) for jb in range(nq) for kh in range(N_KV_HEADS)]
    s_scr[0] = scores(*tiles[0])
    pieces = []
    for n, (jb, kh) in enumerate(tiles):
        win = slice(jb * blk, (jb + 3) * blk)
        if n + 1 < len(tiles):
            s_scr[(n + 1) % 2] = scores(*tiles[n + 1])
        s = s_scr[n % 2]
        s_max = jnp.maximum(jnp.maximum(s[:, :blk], s[:, blk:2 * blk]), s[:, 2 * blk:])
        p, t = [], []
        for r, hh in enumerate((0, 2, 1, 3)):
            rs = slice(r * blk, (r + 1) * blk)
            sink = sink_ref[layer, kh * grp + hh]
            m = jnp.maximum(jnp.max(s_max[rs], axis=-1, keepdims=True), sink)
            p.append(jnp.exp(s[rs] - m).astype(BF16))
            t.append(jnp.exp(sink - m))
        for pr in range(grp // 2):
            ev, od = pr, grp // 2 + pr
            nd = (jnp.dot(p[ev], ve_scr[kh, win, :], preferred_element_type=F32)
                  + jnp.dot(p[od], vo_scr[kh, win, :], preferred_element_type=F32))
            den = nd[:, dk2:] + jnp.where(lane_lo, t[ev], t[od])
            pieces.append(nd[:, :dk2] / den)
        if kh == N_KV_HEADS - 1:
            o = jnp.concatenate(pieces, axis=-1)
            o_ref[0, jb * blk:(jb + 1) * blk, :] = _rms(o, g_ref[...]).astype(BF16)
            pieces = []


def _group_bias(bias):
    grp = N_HEADS // N_KV_HEADS
    bias_g = bias.reshape(N_KV_HEADS, grp, ATT_BLOCK, 3 * ATT_BLOCK)[:, jnp.array([0, 2, 1, 3])]
    return bias_g.reshape(N_KV_HEADS, grp * ATT_BLOCK, 3 * ATT_BLOCK)


def _attention(q, kv, bias_g, sink, g, layer, *, nq=8):
    b, l, d_attn = q.shape
    d_kv2 = kv.shape[-1]
    blk = ATT_BLOCK
    tq = nq * blk
    nb = l // blk
    grp = N_HEADS // N_KV_HEADS
    rows = tq + 2 * blk
    est = (bias_g.size * 4 + 6 * tq * (d_attn + d_kv2) * 2 + N_KV_HEADS * rows * 5 * V7X_LANES * 2
           + 6 * grp * blk * 3 * blk * 4)
    return pl.pallas_call(
        functools.partial(_attn_kernel, nq=nq, layer=layer),
        out_shape=jax.ShapeDtypeStruct((b, l, d_attn), BF16),
        grid=(b, l // tq),
        in_specs=[
            pl.BlockSpec(memory_space=pltpu.SMEM),
            pl.BlockSpec((1, tq, d_attn), lambda bi, i: (bi, i, 0)),
            pl.BlockSpec((1, blk, d_kv2), lambda bi, i: (bi, jnp.maximum(i * nq - 1, 0), 0)),
            pl.BlockSpec((1, tq, d_kv2), lambda bi, i: (bi, i, 0)),
            pl.BlockSpec((1, blk, d_kv2), lambda bi, i: (bi, jnp.minimum((i + 1) * nq, nb - 1), 0)),
            _const_spec(bias_g.shape),
            _layer_spec(g, layer),
        ],
        out_specs=pl.BlockSpec((1, tq, d_attn), lambda bi, i: (bi, i, 0)),
        scratch_shapes=[pltpu.VMEM((N_KV_HEADS, rows, V7X_LANES), BF16),
                        pltpu.VMEM((N_KV_HEADS, rows, 2 * V7X_LANES), BF16),
                        pltpu.VMEM((N_KV_HEADS, rows, 2 * V7X_LANES), BF16),
                        pltpu.VMEM((2, grp * blk, 3 * blk), F32)],
        compiler_params=pltpu.CompilerParams(
            dimension_semantics=("arbitrary", "arbitrary"), vmem_limit_bytes=_vmem_limit(est)),
        name="band_attn",
    )(sink, q, kv, kv, kv, bias_g, g)


def _ssm_kernel(u_ref, a_ref, wb_ref, wc_ref, add_ref, y_ref, ut_scr, yt_scr, x_scr, s_scr, st_scr,
                *, tc, nb, reverse, lane_tile):
    n_half, _, n2 = a_ref.shape
    nre = n2 // 2
    kh = wb_ref.shape[1]
    rows2 = 2 * nb
    n_pairs = tc // 2
    n_slab = ut_scr.shape[0]
    slab_per_half = kh // V7X_LANES

    @pl.when(pl.program_id(0) == 0)
    def _():
        st_scr[...] = jnp.zeros_like(st_scr)

    for bi in range(nb):
        for sl in range(n_slab):
            ut_scr.at[sl][pl.ds(bi, tc, stride=nb), :] = u_ref[bi, :, sl * V7X_LANES:(sl + 1) * V7X_LANES]

    for hf in range(n_half):
        u_h = jnp.concatenate([ut_scr[hf * slab_per_half + k] for k in range(slab_per_half)], axis=-1)
        x_scr[hf] = jnp.dot(u_h.astype(BF16), wb_ref[hf], preferred_element_type=F32)
    for hf in range(n_half):
        for lt in range(nre // lane_tile):
            re_sl = slice(lt * lane_tile, (lt + 1) * lane_tile)
            im_sl = slice(nre + lt * lane_tile, nre + (lt + 1) * lane_tile)
            a_re = a_ref[hf, :, re_sl]
            a_im = a_ref[hf, :, im_sl]

            def step(s_re, s_im, x_re, x_im):
                n_re = a_re * s_re - a_im * s_im + x_re
                n_im = a_re * s_im + a_im * s_re + x_im
                return n_re, n_im

            s_re, s_im = st_scr[hf, :, re_sl], st_scr[hf, :, im_sl]
            for j in range(n_pairs):
                r0 = ((n_pairs - 1 - j) if reverse else j) * rows2
                x_re = x_scr[hf, r0:r0 + rows2, re_sl]
                x_im = x_scr[hf, r0:r0 + rows2, im_sl]
                first, second = (slice(nb, rows2), slice(0, nb)) if reverse else (slice(0, nb), slice(nb, rows2))
                s1_re, s1_im = step(s_re, s_im, x_re[first], x_im[first])
                s_re, s_im = step(s1_re, s1_im, x_re[second], x_im[second])
                lo_re, hi_re = (s_re, s1_re) if reverse else (s1_re, s_re)
                lo_im, hi_im = (s_im, s1_im) if reverse else (s1_im, s_im)
                s_scr[hf, r0:r0 + rows2, re_sl] = jnp.concatenate([lo_re, hi_re], axis=0).astype(BF16)
                s_scr[hf, r0:r0 + rows2, im_sl] = jnp.concatenate([lo_im, hi_im], axis=0).astype(BF16)
            st_scr[hf, :, re_sl] = s_re
            st_scr[hf, :, im_sl] = s_im
        mh = (tc * nb) // 2
        for r in range(2):
            y_h = jnp.dot(s_scr[hf, r * mh:(r + 1) * mh, :], wc_ref[hf], preferred_element_type=F32)
            for k in range(slab_per_half):
                yt_scr[hf * slab_per_half + k, r * mh:(r + 1) * mh, :] = y_h[:, k * V7X_LANES:(k + 1) * V7X_LANES]

    for bi in range(nb):
        for sl in range(n_slab):
            lanes = slice(sl * V7X_LANES, (sl + 1) * V7X_LANES)
            y = yt_scr.at[sl][pl.ds(bi, tc, stride=nb), :]
            if reverse:
                y = y + add_ref[bi, :, lanes]
            else:
                y = y + add_ref[:, lanes] * u_ref[bi, :, lanes]
            y_ref[bi, :, lanes] = y


def _ssm_scan(u, add, a, wb, wc, layer, *, reverse, tc=128, lane_tile=512):
    nb, l, d_ssm = u.shape
    n_half, kh, n2 = wb.shape[2:]
    m = tc * nb
    n_t = l // tc
    di = int(reverse)
    idx = (lambda i: (0, n_t - 1 - i, 0)) if reverse else (lambda i: (0, i, 0))
    est = (6 * m * d_ssm * 4 + n_half * m * n2 * 6
           + 2 * (wb[0, 0].size + wc[0, 0].size) + 4 * a[0, 0].size)
    return pl.pallas_call(
        functools.partial(_ssm_kernel, tc=tc, nb=nb, reverse=reverse, lane_tile=lane_tile),
        out_shape=jax.ShapeDtypeStruct((nb, l, d_ssm), F32),
        grid=(n_t,),
        in_specs=[pl.BlockSpec((nb, tc, d_ssm), idx),
                  _layer_spec(a, layer, di), _layer_spec(wb, layer, di), _layer_spec(wc, layer, di),
                  pl.BlockSpec((nb, tc, d_ssm), idx) if reverse else _layer_spec(add, layer)],
        out_specs=pl.BlockSpec((nb, tc, d_ssm), idx),
        scratch_shapes=[pltpu.VMEM((d_ssm // V7X_LANES, m, V7X_LANES), F32),
                        pltpu.VMEM((d_ssm // V7X_LANES, m, V7X_LANES), F32),
                        pltpu.VMEM((n_half, m, n2), F32),
                        pltpu.VMEM((n_half, m, n2), BF16),
                        pltpu.VMEM((n_half, nb, n2), F32)],
        compiler_params=pltpu.CompilerParams(
            dimension_semantics=("arbitrary",), vmem_limit_bytes=_vmem_limit(est)),
        name="s5_scan_bwd" if reverse else "s5_scan_fwd",
    )(u, a, wb, wc, add)


def _ssm_params(a_re, a_im, log_dt, b_re, b_im, c_re, c_im, nb):
    g, p = a_re.shape
    h = b_re.shape[-1]
    n_half = (g * h) // V7X_MXU_DIM
    gh = g // n_half
    dt = jnp.exp(log_dt)[:, None]
    mag = jnp.exp(dt * a_re)
    ab_re = mag * jnp.cos(dt * a_im)
    ab_im = mag * jnp.sin(dt * a_im)
    den = a_re * a_re + a_im * a_im
    nr, ni = ab_re - 1.0, ab_im
    k_re = (nr * a_re + ni * a_im) / den
    k_im = (ni * a_re - nr * a_im) / den
    bb_re = k_re[..., None] * b_re - k_im[..., None] * b_im
    bb_im = k_re[..., None] * b_im + k_im[..., None] * b_re
    grp_of = lambda n_rows_per_group: jnp.arange(gh * n_rows_per_group) // n_rows_per_group
    in_mask = grp_of(h)[:, None] == grp_of(p)[None, :]
    out_mask = in_mask.T

    def in_proj(bb):
        rows = jnp.swapaxes(bb, 1, 2).reshape(n_half, gh * h, p)
        return jnp.where(in_mask, jnp.tile(rows, (1, 1, gh)), 0.0)

    def out_proj(c):
        rows = jnp.swapaxes(c, 1, 2).reshape(n_half, gh * p, h)
        return jnp.where(out_mask, jnp.tile(rows, (1, 1, gh)), 0.0)

    wb = jnp.concatenate([in_proj(bb_re), in_proj(bb_im)], axis=-1).astype(BF16)
    wc = jnp.concatenate([out_proj(c_re), out_proj(-c_im)], axis=1).astype(BF16)
    a = jnp.concatenate([ab_re.reshape(n_half, gh * p), ab_im.reshape(n_half, gh * p)], axis=-1)
    a = jnp.broadcast_to(a[:, None, :], (n_half, nb, 2 * gh * p))
    return a, wb, wc


def _rows(v):
    return v.astype(F32)[:, None, :]


def _ffn_params(norm, w_gate, w_up, w_down):
    return _rows(norm), w_gate.astype(BF16), w_up.astype(BF16), (0.5 * w_down).astype(BF16)


def kernel(x, rel_bias_table, ffn1_norm, ffn1_w_gate, ffn1_w_up, ffn1_w_down, mix_norm, w_in, attn_sink, ssm_a_re, ssm_a_im, ssm_log_dt, ssm_b_re, ssm_b_im, ssm_c_re, ssm_c_im, ssm_d, ssm_w_glu, ssm_b_glu, attn_out_norm, ssm_out_norm, w_out, ffn2_norm, ffn2_w_gate, ffn2_w_up, ffn2_w_down, final_norm):
    b, l, dm = x.shape
    depth = w_in.shape[0]
    d_attn = N_HEADS * HEAD_DIM
    d_kv2 = 2 * N_KV_HEADS * HEAD_DIM
    assert b == V7X_SUBLANES, "the scan keeps one batch sequence per sublane"

    bias_g = _group_bias(_rel_bias(rel_bias_table.astype(F32)))
    ffn1 = _ffn_params(ffn1_norm, ffn1_w_gate, ffn1_w_up, ffn1_w_down)
    ffn2 = _ffn_params(ffn2_norm, ffn2_w_gate, ffn2_w_up, ffn2_w_down)
    mix_g, w_in_b = _rows(mix_norm), w_in.astype(BF16)
    sink, attn_g = attn_sink.astype(F32), _rows(attn_out_norm)
    ssm = jax.vmap(jax.vmap(functools.partial(_ssm_params, nb=b)))(
        ssm_a_re, ssm_a_im, ssm_log_dt, ssm_b_re, ssm_b_im, ssm_c_re, ssm_c_im)
    skip_d = _rows(ssm_d)
    out_p = (ssm_w_glu.astype(BF16), _rows(ssm_b_glu), _rows(ssm_out_norm), w_out.astype(BF16))
    final_g = final_norm.astype(F32).reshape(1, -1)

    for i in range(depth):
        x = _ffn(x.reshape(b * l, dm), ffn1, i).reshape(b, l, dm)
        q, kv, u = _proj(x, mix_g, w_in_b, i, d_attn=d_attn, d_kv2=d_kv2)
        attn_n = _attention(q, kv, bias_g, sink, attn_g, i)
        y = _ssm_scan(u, skip_d, *ssm, i, reverse=False)
        y = _ssm_scan(u, y, *ssm, i, reverse=True)
        mixer = (attn_n.reshape(b * l, -1), y.reshape(b * l, -1), out_p)
        x = _ffn(x.reshape(b * l, dm), ffn2, i, final_g if i == depth - 1 else None, mixer).reshape(b, l, dm)
    return x
```

```python
import functools
import math

import jax
import jax.numpy as jnp
from jax import lax
from jax.experimental import pallas as pl
from jax.experimental.pallas import tpu as pltpu

EPS = 1e-6
NEG_INF = -1e30

N_HEADS = 8
N_KV_HEADS = 2
HEAD_DIM = 64
SSM_GROUP = 16
SSM_STATE = 64
ATT_BLOCK = 128
N_BUCKETS = 32
MAX_DISTANCE = 128

V7X_SUBLANES = 8
V7X_LANES = 128
V7X_MXU_DIM = 256
V7X_VMEM_BYTES = 64 * 1024 * 1024

F32 = jnp.float32
BF16 = jnp.bfloat16


def _vmem_limit(nbytes):
    return int(min(nbytes + (8 << 20), V7X_VMEM_BYTES - (6 << 20)))


def _rms(x, g):
    return x * lax.rsqrt(jnp.mean(x * x, axis=-1, keepdims=True) + EPS) * g


def _const_spec(shape):
    nd = len(shape)
    return pl.BlockSpec(shape, lambda *_: (0,) * nd, pipeline_mode=pl.Buffered(1))


def _layer_spec(arr, *idx):
    rest = arr.shape[len(idx):]
    return pl.BlockSpec((None,) * len(idx) + rest, lambda *_: idx + (0,) * len(rest),
                        pipeline_mode=pl.Buffered(1))


def _mixer_out(x, at_ref, y_ref, wglu_ref, bglu_ref, gs_ref, wo_ref):
    d_attn = at_ref.shape[-1]
    z = jax.nn.gelu(y_ref[...])
    gate = jnp.dot(z.astype(BF16), wglu_ref[...], preferred_element_type=F32) + bglu_ref[...]
    s_n = _rms(z * jax.nn.sigmoid(gate), gs_ref[...]).astype(BF16)
    return (x + jnp.dot(at_ref[...], wo_ref[:d_attn, :], preferred_element_type=F32)
            + jnp.dot(s_n, wo_ref[d_attn:, :], preferred_element_type=F32))


def _ffn_kernel(x_ref, *refs, fc, final, mixer):
    if mixer:
        mix_refs, refs = refs[:6], refs[6:]
    g_ref, wg_ref, wu_ref, wd_ref = refs[:4]
    if final:
        fg_ref, o_ref, h_scr, gu_scr = refs[4:]
    else:
        o_ref, h_scr, gu_scr = refs[4:]
    n_chunks = wd_ref.shape[0] // fc
    x = x_ref[...]
    if mixer:
        x = _mixer_out(x, *mix_refs)
    h_scr[...] = _rms(x, g_ref[...]).astype(BF16)
    o_ref[...] = x

    def gate_up(c):
        cols = slice(c * fc, (c + 1) * fc)
        gu_scr[c % 2, :, :fc] = jnp.dot(h_scr[...], wg_ref[:, cols], preferred_element_type=F32)
        gu_scr[c % 2, :, fc:] = jnp.dot(h_scr[...], wu_ref[:, cols], preferred_element_type=F32)

    def down(c):
        gu = gu_scr[c % 2]
        gate, up = gu[:, :fc], gu[:, fc:]
        act = (gate * jax.nn.sigmoid(gate) * up).astype(BF16)
        o_ref[...] += jnp.dot(act, wd_ref[c * fc:(c + 1) * fc, :], preferred_element_type=F32)

    gate_up(0)
    for c in range(n_chunks):
        if c + 1 < n_chunks:
            gate_up(c + 1)
        down(c)
    if final:
        o_ref[...] = _rms(o_ref[...], fg_ref[...])


def _ffn(x2, params, layer, final_g=None, mixer=None, *, tm=512, fc=V7X_MXU_DIM):
    t, d = x2.shape
    final = final_g is not None
    row = pl.BlockSpec((tm, d), lambda i: (i, 0))
    in_specs, args = [row], [x2]
    est = 2 * sum(p[0].size for p in params[1:]) + 4 * tm * d * 4 + tm * d * 2 + 4 * tm * fc * 4
    if mixer is not None:
        attn_n, y, mix_params = mixer
        in_specs += [pl.BlockSpec((tm, attn_n.shape[-1]), lambda i: (i, 0)),
                     pl.BlockSpec((tm, y.shape[-1]), lambda i: (i, 0))]
        in_specs += [_layer_spec(p, layer) for p in mix_params]
        args += [attn_n, y, *mix_params]
        est += 2 * (mix_params[0][0].size + mix_params[3][0].size) + 12 * tm * y.shape[-1] * 4
    in_specs += [_layer_spec(p, layer) for p in params]
    fusable = [False] * (len(args) + 1) + [True] * (len(params) - 1)
    args += list(params)
    if final:
        in_specs.append(_const_spec((1, d)))
        args.append(final_g)
        fusable.append(False)
    return pl.pallas_call(
        functools.partial(_ffn_kernel, fc=fc, final=final, mixer=mixer is not None),
        out_shape=jax.ShapeDtypeStruct((t, d), F32),
        grid=(t // tm,),
        in_specs=in_specs,
        out_specs=row,
        scratch_shapes=[pltpu.VMEM((tm, d), BF16), pltpu.VMEM((2, tm, 2 * fc), F32)],
        compiler_params=pltpu.CompilerParams(
            dimension_semantics=("arbitrary",), vmem_limit_bytes=_vmem_limit(est),
            allow_input_fusion=fusable),
        name="mix_ffn" if mixer is not None else "ffn",
    )(*args)


def _proj_kernel(x_ref, g_ref, w_ref, q_ref, kv_ref, u_ref, *, d_attn, d_kv2):
    h = _rms(x_ref[0], g_ref[...]).astype(BF16)
    p = jnp.dot(h, w_ref[...], preferred_element_type=F32)
    q_ref[0] = (p[:, :d_attn] * (HEAD_DIM ** -0.5)).astype(BF16)
    kv_ref[0] = p[:, d_attn:d_attn + d_kv2].astype(BF16)
    u_ref[0] = p[:, d_attn + d_kv2:]


def _proj(x3, g, w_in, layer, *, d_attn, d_kv2, tm=512):
    b, l, d = x3.shape
    d_in = w_in.shape[-1]
    d_ssm = d_in - d_attn - d_kv2
    est = 2 * w_in[0].size + 2 * tm * d * 4 + 3 * tm * d_in * 4
    return pl.pallas_call(
        functools.partial(_proj_kernel, d_attn=d_attn, d_kv2=d_kv2),
        out_shape=(
            jax.ShapeDtypeStruct((b, l, d_attn), BF16),
            jax.ShapeDtypeStruct((b, l, d_kv2), BF16),
            jax.ShapeDtypeStruct((b, l, d_ssm), F32),
        ),
        grid=(b, l // tm),
        in_specs=[pl.BlockSpec((1, tm, d), lambda bi, i: (bi, i, 0)),
                  _layer_spec(g, layer), _layer_spec(w_in, layer)],
        out_specs=(
            pl.BlockSpec((1, tm, d_attn), lambda bi, i: (bi, i, 0)),
            pl.BlockSpec((1, tm, d_kv2), lambda bi, i: (bi, i, 0)),
            pl.BlockSpec((1, tm, d_ssm), lambda bi, i: (bi, i, 0)),
        ),
        compiler_params=pltpu.CompilerParams(
            dimension_semantics=("arbitrary", "arbitrary"), vmem_limit_bytes=_vmem_limit(est)),
        name="mix_proj",
    )(x3, g, w_in)


def _bias_kernel(table_ref, idx_ref, band_ref, o_ref):
    idx = idx_ref[...]
    band = band_ref[...] > 0
    for h in range(N_HEADS):
        acc = jnp.zeros(idx.shape, F32)
        for bkt in range(N_BUCKETS):
            acc = jnp.where(idx == bkt, table_ref[bkt, h], acc)
        o_ref[h] = jnp.where(band, acc, NEG_INF)


def _t5_bucket(rel):
    half = N_BUCKETS // 2
    max_exact = half // 2
    ret = jnp.where(rel > 0, half, 0)
    n = jnp.abs(rel)
    nf = jnp.maximum(n, 1).astype(F32)
    large = max_exact + (jnp.log(nf / max_exact) / math.log(MAX_DISTANCE / max_exact)
                         * (half - max_exact)).astype(jnp.int32)
    large = jnp.minimum(large, half - 1)
    return ret + jnp.where(n < max_exact, n, large)


def _rel_bias(table):
    rel = (jnp.arange(3 * ATT_BLOCK)[None, :] - ATT_BLOCK) - jnp.arange(ATT_BLOCK)[:, None]
    idx = _t5_bucket(rel).astype(jnp.int32)
    band = (jnp.abs(rel) <= ATT_BLOCK).astype(jnp.int32)
    return pl.pallas_call(
        _bias_kernel,
        out_shape=jax.ShapeDtypeStruct((N_HEADS, ATT_BLOCK, 3 * ATT_BLOCK), F32),
        in_specs=[pl.BlockSpec(memory_space=pltpu.SMEM),
                  pl.BlockSpec(memory_space=pltpu.VMEM),
                  pl.BlockSpec(memory_space=pltpu.VMEM)],
        out_specs=pl.BlockSpec(memory_space=pltpu.VMEM),
        name="rel_bias",
    )(table, idx, band)


def _attn_kernel(sink_ref, q_ref, kvp_ref, kvm_ref, kvn_ref, bias_ref, g_ref, o_ref,
                 kd_scr, ve_scr, vo_scr, s_scr, *, nq, layer):
    i = pl.program_id(1)
    n_i = pl.num_programs(1)
    blk = ATT_BLOCK
    dh = HEAD_DIM
    dk2 = N_KV_HEADS * dh
    grp = N_HEADS // N_KV_HEADS
    assert dk2 == V7X_LANES and grp == 4
    lane = lax.broadcasted_iota(jnp.int32, (1, dk2), 1)
    m_lo = (lane < dh).astype(BF16)
    m_hi = (lane >= dh).astype(BF16)

    kv = jnp.concatenate([kvp_ref[0], kvm_ref[0], kvn_ref[0]], axis=0)
    rows = kv.shape[0]
    k_row, v_row = kv[:, :dk2], kv[:, dk2:]
    k_swap = jnp.concatenate([k_row[:, dh:], k_row[:, :dh]], axis=-1)
    v_swap = jnp.concatenate([v_row[:, dh:], v_row[:, :dh]], axis=-1)
    ones_lo = jnp.broadcast_to(m_lo, (rows, dk2))
    ones_hi = jnp.broadcast_to(m_hi, (rows, dk2))
    kd_scr[0] = k_row * m_lo + k_swap * m_hi
    kd_scr[1] = k_swap * m_lo + k_row * m_hi
    ve_scr[0] = jnp.concatenate([v_row * m_lo, ones_lo], axis=-1)
    vo_scr[0] = jnp.concatenate([v_swap * m_hi, ones_hi], axis=-1)
    ve_scr[1] = jnp.concatenate([v_swap * m_lo, ones_lo], axis=-1)
    vo_scr[1] = jnp.concatenate([v_row * m_hi, ones_hi], axis=-1)

    col = lax.broadcasted_iota(jnp.int32, (1, 3 * blk), 1)
    edge_first = jnp.where(col < blk, jnp.where(i == 0, NEG_INF, 0.0), 0.0)
    edge_last = jnp.where(col >= 2 * blk, jnp.where(i == n_i - 1, NEG_INF, 0.0), 0.0)
    lane_lo = lane < dh

    def scores(jb, kh):
        qs = []
        for msk in (m_lo, m_hi):
            for pr in range(grp // 2):
                c0 = kh * grp * dh + pr * dk2
                qs.append(q_ref[0, jb * blk:(jb + 1) * blk, c0:c0 + dk2] * msk)
        q_st = jnp.concatenate(qs, axis=0)
        s = lax.dot_general(q_st, kd_scr[kh, jb * blk:(jb + 3) * blk, :], (((1,), (1,)), ((), ())),
                            preferred_element_type=F32)
        s = s + bias_ref[kh]
        if jb == 0:
            s = s + edge_first
        if jb == nq - 1:
            s = s + edge_last
        return s

    tiles = [(jb, kh) for jb in range(nq) for kh in range(N_KV_HEADS)]
    s_scr[0] = scores(*tiles[0])
    pieces = []
    for n, (jb, kh) in enumerate(tiles):
        win = slice(jb * blk, (jb + 3) * blk)
        if n + 1 < len(tiles):
            s_scr[(n + 1) % 2] = scores(*tiles[n + 1])
        s = s_scr[n % 2]
        s_max = jnp.maximum(jnp.maximum(s[:, :blk], s[:, blk:2 * blk]), s[:, 2 * blk:])
        p, t = [], []
        for r, hh in enumerate((0, 2, 1, 3)):
            rs = slice(r * blk, (r + 1) * blk)
            sink = sink_ref[layer, kh * grp + hh]
            m = jnp.maximum(jnp.max(s_max[rs], axis=-1, keepdims=True), sink)
            p.append(jnp.exp(s[rs] - m).astype(BF16))
            t.append(jnp.exp(sink - m))
        for pr in range(grp // 2):
            ev, od = pr, grp // 2 + pr
            nd = (jnp.dot(p[ev], ve_scr[kh, win, :], preferred_element_type=F32)
                  + jnp.dot(p[od], vo_scr[kh, win, :], preferred_element_type=F32))
            den = nd[:, dk2:] + jnp.where(lane_lo, t[ev], t[od])
            pieces.append(nd[:, :dk2] / den)
        if kh == N_KV_HEADS - 1:
            o = jnp.concatenate(pieces, axis=-1)
            o_ref[0, jb * blk:(jb + 1) * blk, :] = _rms(o, g_ref[...]).astype(BF16)
            pieces = []


def _group_bias(bias):
    grp = N_HEADS // N_KV_HEADS
    bias_g = bias.reshape(N_KV_HEADS, grp, ATT_BLOCK, 3 * ATT_BLOCK)[:, jnp.array([0, 2, 1, 3])]
    return bias_g.reshape(N_KV_HEADS, grp * ATT_BLOCK, 3 * ATT_BLOCK)


def _attention(q, kv, bias_g, sink, g, layer, *, nq=8):
    b, l, d_attn = q.shape
    d_kv2 = kv.shape[-1]
    blk = ATT_BLOCK
    tq = nq * blk
    nb = l // blk
    grp = N_HEADS // N_KV_HEADS
    rows = tq + 2 * blk
    est = (bias_g.size * 4 + 6 * tq * (d_attn + d_kv2) * 2 + N_KV_HEADS * rows * 5 * V7X_LANES * 2
           + 6 * grp * blk * 3 * blk * 4)
    return pl.pallas_call(
        functools.partial(_attn_kernel, nq=nq, layer=layer),
        out_shape=jax.ShapeDtypeStruct((b, l, d_attn), BF16),
        grid=(b, l // tq),
        in_specs=[
            pl.BlockSpec(memory_space=pltpu.SMEM),
            pl.BlockSpec((1, tq, d_attn), lambda bi, i: (bi, i, 0)),
            pl.BlockSpec((1, blk, d_kv2), lambda bi, i: (bi, jnp.maximum(i * nq - 1, 0), 0)),
            pl.BlockSpec((1, tq, d_kv2), lambda bi, i: (bi, i, 0)),
            pl.BlockSpec((1, blk, d_kv2), lambda bi, i: (bi, jnp.minimum((i + 1) * nq, nb - 1), 0)),
            _const_spec(bias_g.shape),
            _layer_spec(g, layer),
        ],
        out_specs=pl.BlockSpec((1, tq, d_attn), lambda bi, i: (bi, i, 0)),
        scratch_shapes=[pltpu.VMEM((N_KV_HEADS, rows, V7X_LANES), BF16),
                        pltpu.VMEM((N_KV_HEADS, rows, 2 * V7X_LANES), BF16),
                        pltpu.VMEM((N_KV_HEADS, rows, 2 * V7X_LANES), BF16),
                        pltpu.VMEM((2, grp * blk, 3 * blk), F32)],
        compiler_params=pltpu.CompilerParams(
            dimension_semantics=("arbitrary", "arbitrary"), vmem_limit_bytes=_vmem_limit(est)),
        name="band_attn",
    )(sink, q, kv, kv, kv, bias_g, g)


def _ssm_kernel(u_ref, a_ref, wb_ref, wc_ref, add_ref, y_ref, ut_scr, yt_scr, x_scr, s_scr, st_scr,
                *, tc, nb, reverse, lane_tile):
    n_half, _, n2 = a_ref.shape
    nre = n2 // 2
    kh = wb_ref.shape[1]
    rows2 = 2 * nb
    n_pairs = tc // 2
    n_slab = ut_scr.shape[0]
    slab_per_half = kh // V7X_LANES

    @pl.when(pl.program_id(0) == 0)
    def _():
        st_scr[...] = jnp.zeros_like(st_scr)

    for bi in range(nb):
        for sl in range(n_slab):
            ut_scr.at[sl][pl.ds(bi, tc, stride=nb), :] = u_ref[bi, :, sl * V7X_LANES:(sl + 1) * V7X_LANES]

    for hf in range(n_half):
        u_h = jnp.concatenate([ut_scr[hf * slab_per_half + k] for k in range(slab_per_half)], axis=-1)
        x_scr[hf] = jnp.dot(u_h.astype(BF16), wb_ref[hf], preferred_element_type=F32)
    for hf in range(n_half):
        for lt in range(nre // lane_tile):
            re_sl = slice(lt * lane_tile, (lt + 1) * lane_tile)
            im_sl = slice(nre + lt * lane_tile, nre + (lt + 1) * lane_tile)
            a_re = a_ref[hf, :, re_sl]
            a_im = a_ref[hf, :, im_sl]

            def step(s_re, s_im, x_re, x_im):
                n_re = a_re * s_re - a_im * s_im + x_re
                n_im = a_re * s_im + a_im * s_re + x_im
                return n_re, n_im

            s_re, s_im = st_scr[hf, :, re_sl], st_scr[hf, :, im_sl]
            for j in range(n_pairs):
                r0 = ((n_pairs - 1 - j) if reverse else j) * rows2
                x_re = x_scr[hf, r0:r0 + rows2, re_sl]
                x_im = x_scr[hf, r0:r0 + rows2, im_sl]
                first, second = (slice(nb, rows2), slice(0, nb)) if reverse else (slice(0, nb), slice(nb, rows2))
                s1_re, s1_im = step(s_re, s_im, x_re[first], x_im[first])
                s_re, s_im = step(s1_re, s1_im, x_re[second], x_im[second])
                lo_re, hi_re = (s_re, s1_re) if reverse else (s1_re, s_re)
                lo_im, hi_im = (s_im, s1_im) if reverse else (s1_im, s_im)
                s_scr[hf, r0:r0 + rows2, re_sl] = jnp.concatenate([lo_re, hi_re], axis=0).astype(BF16)
                s_scr[hf, r0:r0 + rows2, im_sl] = jnp.concatenate([lo_im, hi_im], axis=0).astype(BF16)
            st_scr[hf, :, re_sl] = s_re
            st_scr[hf, :, im_sl] = s_im
        mh = (tc * nb) // 2
        for r in range(2):
            y_h = jnp.dot(s_scr[hf, r * mh:(r + 1) * mh, :], wc_ref[hf], preferred_element_type=F32)
            for k in range(slab_per_half):
                yt_scr[hf * slab_per_half + k, r * mh:(r + 1) * mh, :] = y_h[:, k * V7X_LANES:(k + 1) * V7X_LANES]

    for bi in range(nb):
        for sl in range(n_slab):
            lanes = slice(sl * V7X_LANES, (sl + 1) * V7X_LANES)
            y = yt_scr.at[sl][pl.ds(bi, tc, stride=nb), :]
            if reverse:
                y = y + add_ref[bi, :, lanes]
            else:
                y = y + add_ref[:, lanes] * u_ref[bi, :, lanes]
            y_ref[bi, :, lanes] = y


def _ssm_scan(u, add, a, wb, wc, layer, *, reverse, tc=128, lane_tile=512):
    nb, l, d_ssm = u.shape
    n_half, kh, n2 = wb.shape[2:]
    m = tc * nb
    n_t = l // tc
    di = int(reverse)
    idx = (lambda i: (0, n_t - 1 - i, 0)) if reverse else (lambda i: (0, i, 0))
    est = (6 * m * d_ssm * 4 + n_half * m * n2 * 6
           + 2 * (wb[0, 0].size + wc[0, 0].size) + 4 * a[0, 0].size)
    return pl.pallas_call(
        functools.partial(_ssm_kernel, tc=tc, nb=nb, reverse=reverse, lane_tile=lane_tile),
        out_shape=jax.ShapeDtypeStruct((nb, l, d_ssm), F32),
        grid=(n_t,),
        in_specs=[pl.BlockSpec((nb, tc, d_ssm), idx),
                  _layer_spec(a, layer, di), _layer_spec(wb, layer, di), _layer_spec(wc, layer, di),
                  pl.BlockSpec((nb, tc, d_ssm), idx) if reverse else _layer_spec(add, layer)],
        out_specs=pl.BlockSpec((nb, tc, d_ssm), idx),
        scratch_shapes=[pltpu.VMEM((d_ssm // V7X_LANES, m, V7X_LANES), F32),
                        pltpu.VMEM((d_ssm // V7X_LANES, m, V7X_LANES), F32),
                        pltpu.VMEM((n_half, m, n2), F32),
                        pltpu.VMEM((n_half, m, n2), BF16),
                        pltpu.VMEM((n_half, nb, n2), F32)],
        compiler_params=pltpu.CompilerParams(
            dimension_semantics=("arbitrary",), vmem_limit_bytes=_vmem_limit(est)),
        name="s5_scan_bwd" if reverse else "s5_scan_fwd",
    )(u, a, wb, wc, add)


def _ssm_params(a_re, a_im, log_dt, b_re, b_im, c_re, c_im, nb):
    g, p = a_re.shape
    h = b_re.shape[-1]
    n_half = (g * h) // V7X_MXU_DIM
    gh = g // n_half
    dt = jnp.exp(log_dt)[:, None]
    mag = jnp.exp(dt * a_re)
    ab_re = mag * jnp.cos(dt * a_im)
    ab_im = mag * jnp.sin(dt * a_im)
    den = a_re * a_re + a_im * a_im
    nr, ni = ab_re - 1.0, ab_im
    k_re = (nr * a_re + ni * a_im) / den
    k_im = (ni * a_re - nr * a_im) / den
    bb_re = k_re[..., None] * b_re - k_im[..., None] * b_im
    bb_im = k_re[..., None] * b_im + k_im[..., None] * b_re
    grp_of = lambda n_rows_per_group: jnp.arange(gh * n_rows_per_group) // n_rows_per_group
    in_mask = grp_of(h)[:, None] == grp_of(p)[None, :]
    out_mask = in_mask.T

    def in_proj(bb):
        rows = jnp.swapaxes(bb, 1, 2).reshape(n_half, gh * h, p)
        return jnp.where(in_mask, jnp.tile(rows, (1, 1, gh)), 0.0)

    def out_proj(c):
        rows = jnp.swapaxes(c, 1, 2).reshape(n_half, gh * p, h)
        return jnp.where(out_mask, jnp.tile(rows, (1, 1, gh)), 0.0)

    wb = jnp.concatenate([in_proj(bb_re), in_proj(bb_im)], axis=-1).astype(BF16)
    wc = jnp.concatenate([out_proj(c_re), out_proj(-c_im)], axis=1).astype(BF16)
    a = jnp.concatenate([ab_re.reshape(n_half, gh * p), ab_im.reshape(n_half, gh * p)], axis=-1)
    a = jnp.broadcast_to(a[:, None, :], (n_half, nb, 2 * gh * p))
    return a, wb, wc


def _rows(v):
    return v.astype(F32)[:, None, :]


def _ffn_params(norm, w_gate, w_up, w_down):
    return _rows(norm), w_gate.astype(BF16), w_up.astype(BF16), (0.5 * w_down).astype(BF16)


def kernel(x, rel_bias_table, ffn1_norm, ffn1_w_gate, ffn1_w_up, ffn1_w_down, mix_norm, w_in, attn_sink, ssm_a_re, ssm_a_im, ssm_log_dt, ssm_b_re, ssm_b_im, ssm_c_re, ssm_c_im, ssm_d, ssm_w_glu, ssm_b_glu, attn_out_norm, ssm_out_norm, w_out, ffn2_norm, ffn2_w_gate, ffn2_w_up, ffn2_w_down, final_norm):
    b, l, dm = x.shape
    depth = w_in.shape[0]
    d_attn = N_HEADS * HEAD_DIM
    d_kv2 = 2 * N_KV_HEADS * HEAD_DIM
    assert b == V7X_SUBLANES, "the scan keeps one batch sequence per sublane"

    bias_g = _group_bias(_rel_bias(rel_bias_table.astype(F32)))
    ffn1 = _ffn_params(ffn1_norm, ffn1_w_gate, ffn1_w_up, ffn1_w_down)
    ffn2 = _ffn_params(ffn2_norm, ffn2_w_gate, ffn2_w_up, ffn2_w_down)
    mix_g, w_in_b = _rows(mix_norm), w_in.astype(BF16)
    sink, attn_g = attn_sink.astype(F32), _rows(attn_out_norm)
    ssm = jax.vmap(jax.vmap(functools.partial(_ssm_params, nb=b)))(
        ssm_a_re, ssm_a_im, ssm_log_dt, ssm_b_re, ssm_b_im, ssm_c_re, ssm_c_im)
    skip_d = _rows(ssm_d)
    out_p = (ssm_w_glu.astype(BF16), _rows(ssm_b_glu), _rows(ssm_out_norm), w_out.astype(BF16))
    final_g = final_norm.astype(F32).reshape(1, -1)

    for i in range(depth):
        x = _ffn(x.reshape(b * l, dm), ffn1, i).reshape(b, l, dm)
        q, kv, u = _proj(x, mix_g, w_in_b, i, d_attn=d_attn, d_kv2=d_kv2)
        attn_n = _attention(q, kv, bias_g, sink, attn_g, i)
        y = _ssm_scan(u, skip_d, *ssm, i, reverse=False)
        y = _ssm_scan(u, y, *ssm, i, reverse=True)
        mixer = (attn_n.reshape(b * l, -1), y.reshape(b * l, -1), out_p)
        x = _ffn(x.reshape(b * l, dm), ffn2, i, final_g if i == depth - 1 else None, mixer).reshape(b, l, dm)
    return x
```
